```python
import math
import jax
import jax.numpy as jnp
from jax import lax
import numpy as np

D_MODEL = 1024
BATCH = 2
SEQ = 16384
DEPTH = 4

GRID_W = 64
CTX_LEN = 256
N_MIXERS = 3
EPS = 1e-6

ATT_HEADS = 8
ATT_KV_HEADS = 2
HEAD_DIM = D_MODEL // ATT_HEADS
ATT_GROUP = ATT_HEADS // ATT_KV_HEADS
Q_BLOCK = 128
ROPE_THETA = 10000.0

SSM_WIDTH = D_MODEL // 2
SSM_GROUP = 16
SSM_GROUPS = SSM_WIDTH // SSM_GROUP
SSM_STATE = 64
DT_MIN = 1e-3
DT_MAX = 1e-1

LRU_WIDTH = D_MODEL
LRU_BLOCKS = 8
LRU_BLOCK = LRU_WIDTH // LRU_BLOCKS
CONV_W = 4
CONV_LEFT = 2
LRU_C = 8.0

PEER_HEADS = 8
N_KEYS = 128
N_EXPERTS = N_KEYS * N_KEYS
PEER_KEY_DIM = 256
PEER_TOPK = 16
PEER_CHUNK = 128

N_ATTN = (DEPTH + 2) // 3
N_SSM = (DEPTH + 1) // 3
N_LRU = DEPTH // 3

kernel_name = 'hybrid_dit_gqa_s5_rglru_peer'


def rmsnorm(x, g):
    xf = x.astype(jnp.float32)
    y = xf * lax.rsqrt(jnp.mean(xf * xf, axis=-1, keepdims=True) + EPS)
    return (y * g.astype(jnp.float32)).astype(x.dtype)


def modulate(h, shift, scale):
    return h * (1.0 + scale) + shift


def axial_rope_tables(n_tokens):
    rows = n_tokens // GRID_W
    r_idx, c_idx = jnp.meshgrid(jnp.arange(rows, dtype=jnp.float32),
                                jnp.arange(GRID_W, dtype=jnp.float32), indexing='ij')
    pairs_per_axis = HEAD_DIM // 4
    freqs = ROPE_THETA ** (-jnp.arange(pairs_per_axis, dtype=jnp.float32) / pairs_per_axis)
    ang = jnp.concatenate([r_idx.reshape(-1, 1) * freqs, c_idx.reshape(-1, 1) * freqs], axis=-1)
    return jnp.cos(ang), jnp.sin(ang)


def apply_rope(x, cos, sin):
    shp = (1, cos.shape[0]) + (1,) * (x.ndim - 3) + (cos.shape[1],)
    cs, sn = cos.reshape(shp), sin.reshape(shp)
    xf = x.astype(jnp.float32).reshape(x.shape[:-1] + (HEAD_DIM // 2, 2))
    x0, x1 = xf[..., 0], xf[..., 1]
    out = jnp.stack([x0 * cs - x1 * sn, x0 * sn + x1 * cs], axis=-1)
    return out.reshape(x.shape).astype(x.dtype)


def _combine(left, right):
    a_l, b_l = left
    a_r, b_r = right
    return a_l * a_r, a_r * b_l + b_r


def linear_scan(a, b, h0, reverse):
    a_cum, b_cum = lax.associative_scan(_combine, (a, b), reverse=reverse, axis=1)
    if h0 is None:
        return b_cum
    return b_cum + a_cum * h0[:, None]


def attention_mixer(h_ctx, h_lat, w_in, q_gain, k_gain, w_out, cos, sin, ctx_out):
    qd = ATT_HEADS * HEAD_DIM
    bsz, s_lat = h_lat.shape[:2]

    def project_q(h):
        q = (h @ w_in[:, :qd]).reshape(h.shape[:2] + (ATT_KV_HEADS, ATT_GROUP, HEAD_DIM))
        return rmsnorm(q, q_gain)

    def project_kv(h):
        k, v = jnp.split(h @ w_in[:, qd:], 2, axis=-1)
        k = rmsnorm(k.reshape(h.shape[:2] + (ATT_KV_HEADS, HEAD_DIM)), k_gain)
        return k, v.reshape(h.shape[:2] + (ATT_KV_HEADS, HEAD_DIM))

    def attend(q, k, v):
        s = jnp.einsum('btkgd,bskd->bkgts', q, k, preferred_element_type=jnp.float32) * (HEAD_DIM ** -0.5)
        p = jax.nn.softmax(s, axis=-1).astype(v.dtype)
        return jnp.einsum('bkgts,bskd->btkgd', p, v)

    k_c, v_c = project_kv(h_ctx)
    k_l, v_l = project_kv(h_lat)
    k_l = apply_rope(k_l, cos, sin)
    q_l = apply_rope(project_q(h_lat), cos, sin)
    keys = jnp.concatenate([k_c, k_l], axis=1)
    vals = jnp.concatenate([v_c, v_l], axis=1)
    nblk = s_lat // Q_BLOCK
    q_blocks = q_l.reshape(bsz, nblk, Q_BLOCK, ATT_KV_HEADS, ATT_GROUP, HEAD_DIM).transpose(1, 0, 2, 3, 4, 5)
    o = lax.map(lambda qb: attend(qb, keys, vals), q_blocks)
    o = o.transpose(1, 0, 2, 3, 4, 5).reshape(bsz, s_lat, qd)
    y_lat = o @ w_out
    y_ctx = None
    if ctx_out:
        o_c = attend(project_q(h_ctx), k_c, v_c)
        y_ctx = o_c.reshape(h_ctx.shape[:2] + (qd,)) @ w_out
    return y_ctx, y_lat


def s5_mixer(h_ctx, h_lat, w_in, lam_re, lam_im, log_dt, b_re, b_im, c_re, c_im, d_skip, w_glu, w_out, ctx_out):
    f32 = jnp.float32
    u_c = (h_ctx @ w_in).astype(f32)
    u_l = (h_lat @ w_in).astype(f32)

    def drive(u, b_bar):
        ug = u.reshape(u.shape[:2] + (SSM_GROUPS, SSM_GROUP)).astype(jnp.complex64)
        return jnp.einsum('gpc,btgc->btgp', b_bar, ug)

    def readout(c_mat, st):
        y = jnp.real(jnp.einsum('gcp,btgp->btgc', c_mat, st))
        return y.reshape(y.shape[:2] + (SSM_WIDTH,))

    y_l = d_skip.astype(f32) * u_l
    y_c = d_skip.astype(f32) * u_c if ctx_out else None
    for d in range(2):
        rev = d == 1
        lam = lax.complex(lam_re[d].astype(f32), lam_im[d].astype(f32))
        dt = jnp.exp(log_dt[d].astype(f32))[:, None]
        lam_bar = jnp.exp(lam * dt)
        b_bar = ((lam_bar - 1.0) / lam)[..., None] * lax.complex(b_re[d].astype(f32), b_im[d].astype(f32))
        c_mat = lax.complex(c_re[d].astype(f32), c_im[d].astype(f32))
        a_c = jnp.broadcast_to(lam_bar, (1, u_c.shape[1]) + lam_bar.shape)
        st_c = linear_scan(a_c, drive(u_c, b_bar), None, rev)
        h0 = st_c[:, 0] if rev else st_c[:, -1]
        a_l = jnp.broadcast_to(lam_bar, (1, u_l.shape[1]) + lam_bar.shape)
        st_l = linear_scan(a_l, drive(u_l, b_bar), h0, rev)
        y_l = y_l + readout(c_mat, st_l)
        if ctx_out:
            y_c = y_c + readout(c_mat, st_c)

    def glu_out(y, dtype):
        z = jax.nn.gelu(y, approximate=False).astype(dtype)
        z = z * jax.nn.sigmoid(z @ w_glu)
        return z @ w_out

    y_lat = glu_out(y_l, h_lat.dtype)
    y_ctx = glu_out(y_c, h_ctx.dtype) if ctx_out else None
    return y_ctx, y_lat


def centred_depthwise_conv(x, w, b):
    y = lax.conv_general_dilated(x, w[:, None, :], window_strides=(1,),
                                 padding=[(CONV_LEFT, CONV_W - 1 - CONV_LEFT)],
                                 dimension_numbers=('NWC', 'WIO', 'NWC'),
                                 feature_group_count=x.shape[-1])
    return y + b


def rglru_mixer(h_ctx, h_lat, w_in, conv_w, conv_b, w_rg, b_rg, w_ig, b_ig, lam, w_out, ctx_out):
    f32 = jnp.float32

    def branches(h):
        gate, xr = jnp.split(h @ w_in, 2, axis=-1)
        return gate, centred_depthwise_conv(xr, conv_w, conv_b).astype(f32)

    gate_c, xr_c = branches(h_ctx)
    gate_l, xr_l = branches(h_lat)

    def block_diag(xr, w, b):
        xb = xr.reshape(xr.shape[:2] + (LRU_BLOCKS, LRU_BLOCK))
        return jnp.einsum('btnc,ncd->btnd', xb, w.astype(f32)).reshape(xr.shape) + b.astype(f32)

    def coeffs(xr, d):
        r = jax.nn.sigmoid(block_diag(xr, w_rg[d], b_rg[d]))
        i = jax.nn.sigmoid(block_diag(xr, w_ig[d], b_ig[d]))
        log_a = -LRU_C * r * jax.nn.softplus(-lam[d].astype(f32))
        a = jnp.exp(log_a)
        b = jnp.sqrt(-jnp.expm1(2.0 * log_a)) * (i * xr)
        return a, b

    def run_direction(d):
        rev = d == 1
        a_c, b_c = coeffs(xr_c, d)
        st_c = linear_scan(a_c, b_c, None, rev)
        h0 = st_c[:, 0] if rev else st_c[:, -1]
        a_l, b_l = coeffs(xr_l, d)
        return st_c, linear_scan(a_l, b_l, h0, rev)

    st_c_f, st_l_f = run_direction(0)
    st_c_b, st_l_b = run_direction(1)

    def out(st, gate):
        y = st * jax.nn.gelu(gate.astype(f32), approximate=False)
        return y.astype(gate.dtype) @ w_out

    y_lat = out(st_l_f + st_l_b, gate_l)
    y_ctx = out(st_c_f + st_c_b, gate_c) if ctx_out else None
    return y_ctx, y_lat


def peer_ffn(h, w_q, sub_k1, sub_k2, u_tab, v_tab):
    bsz, t_len, d = h.shape
    flat = h.reshape(-1, PEER_CHUNK, d)

    def one(hc):
        q = (hc @ w_q).reshape(PEER_CHUNK, PEER_HEADS, 2, PEER_KEY_DIM // 2)
        s1 = jnp.einsum('thd,hnd->thn', q[:, :, 0], sub_k1, preferred_element_type=jnp.float32)
        s2 = jnp.einsum('thd,hnd->thn', q[:, :, 1], sub_k2, preferred_element_type=jnp.float32)
        v1, i1 = lax.top_k(s1, PEER_TOPK)
        v2, i2 = lax.top_k(s2, PEER_TOPK)
        cand = (v1[..., :, None] + v2[..., None, :]).reshape(PEER_CHUNK, PEER_HEADS, PEER_TOPK * PEER_TOPK)
        cidx = (i1[..., :, None] * N_KEYS + i2[..., None, :]).reshape(PEER_CHUNK, PEER_HEADS, PEER_TOPK * PEER_TOPK)
        best, pos = lax.top_k(cand, PEER_TOPK)
        idx = jnp.take_along_axis(cidx, pos, axis=-1)
        g = jax.nn.softmax(best, axis=-1)
        act = jax.nn.gelu(jnp.einsum('thkd,td->thk', u_tab[idx], hc, preferred_element_type=jnp.float32),
                          approximate=False)
        wts = (g * act).astype(hc.dtype)
        return jnp.einsum('thk,thkd->td', wts, v_tab[idx])

    return lax.map(one, flat).reshape(bsz, t_len, d)


def setup_inputs(seed: int = 0) -> dict:
    key = jax.random.key(seed)
    keys = iter(jax.random.split(key, 48))
    f32 = jnp.float32

    def normal(shape, scale):
        return jax.random.normal(next(keys), shape, f32) * scale

    def gain(shape):
        return 1.0 + 0.02 * jax.random.normal(next(keys), shape, f32)

    d = D_MODEL
    qkv_width = (ATT_HEADS + 2 * ATT_KV_HEADS) * HEAD_DIM
    n_state = jnp.arange(SSM_STATE, dtype=f32)
    inp = {}
    inp['x'] = normal((BATCH, SEQ, d), 1.0)
    inp['c'] = normal((BATCH, d), 1.0)
    inp['ctx'] = normal((BATCH, CTX_LEN, d), 1.0)
    inp['c_ctx'] = normal((d,), 1.0)
    inp['mod_w'] = normal((DEPTH, d, 6 * d), 0.5 * d ** -0.5)
    inp['mod_b'] = normal((DEPTH, 6 * d), 0.02)
    inp['norm_mix'] = gain((DEPTH, d))
    inp['norm_ffn'] = gain((DEPTH, d))
    inp['norm_final'] = gain((d,))
    inp['attn_w_in'] = normal((N_ATTN, d, qkv_width), d ** -0.5)
    inp['attn_q_gain'] = gain((N_ATTN, HEAD_DIM))
    inp['attn_k_gain'] = gain((N_ATTN, HEAD_DIM))
    inp['attn_w_out'] = normal((N_ATTN, ATT_HEADS * HEAD_DIM, d), (ATT_HEADS * HEAD_DIM) ** -0.5)
    inp['ssm_w_in'] = normal((N_SSM, d, SSM_WIDTH), d ** -0.5)
    inp['ssm_lam_re'] = -0.5 + normal((N_SSM, 2, SSM_GROUPS, SSM_STATE), 0.01)
    inp['ssm_lam_im'] = math.pi * n_state + normal((N_SSM, 2, SSM_GROUPS, SSM_STATE), 0.01)
    inp['ssm_log_dt'] = jax.random.uniform(next(keys), (N_SSM, 2, SSM_GROUPS), f32,
                                           math.log(DT_MIN), math.log(DT_MAX))
    b_scale = (2.0 * SSM_GROUP) ** -0.5
    c_scale = (2.0 * SSM_STATE) ** -0.5
    inp['ssm_b_re'] = normal((N_SSM, 2, SSM_GROUPS, SSM_STATE, SSM_GROUP), b_scale)
    inp['ssm_b_im'] = normal((N_SSM, 2, SSM_GROUPS, SSM_STATE, SSM_GROUP), b_scale)
    inp['ssm_c_re'] = normal((N_SSM, 2, SSM_GROUPS, SSM_GROUP, SSM_STATE), c_scale)
    inp['ssm_c_im'] = normal((N_SSM, 2, SSM_GROUPS, SSM_GROUP, SSM_STATE), c_scale)
    inp['ssm_d'] = normal((N_SSM, SSM_WIDTH), 1.0)
    inp['ssm_w_glu'] = normal((N_SSM, SSM_WIDTH, SSM_WIDTH), SSM_WIDTH ** -0.5)
    inp['ssm_w_out'] = normal((N_SSM, SSM_WIDTH, d), SSM_WIDTH ** -0.5)
    inp['lru_w_in'] = normal((N_LRU, d, 2 * LRU_WIDTH), d ** -0.5)
    inp['lru_conv_w'] = normal((N_LRU, CONV_W, LRU_WIDTH), CONV_W ** -0.5)
    inp['lru_conv_b'] = normal((N_LRU, LRU_WIDTH), 0.01)
    inp['lru_w_rg'] = normal((N_LRU, 2, LRU_BLOCKS, LRU_BLOCK, LRU_BLOCK), LRU_BLOCK ** -0.5)
    inp['lru_b_rg'] = normal((N_LRU, 2, LRU_WIDTH), 0.01)
    inp['lru_w_ig'] = normal((N_LRU, 2, LRU_BLOCKS, LRU_BLOCK, LRU_BLOCK), LRU_BLOCK ** -0.5)
    inp['lru_b_ig'] = normal((N_LRU, 2, LRU_WIDTH), 0.01)
    a_c = jax.random.uniform(next(keys), (N_LRU, 2, LRU_WIDTH), f32, 0.9, 0.999)
    a_base = a_c ** (1.0 / LRU_C)
    inp['lru_lam'] = jnp.log(a_base) - jnp.log1p(-a_base)
    inp['lru_w_out'] = normal((N_LRU, LRU_WIDTH, d), LRU_WIDTH ** -0.5)
    inp['peer_w_q'] = normal((DEPTH, d, PEER_HEADS * PEER_KEY_DIM), d ** -0.5)
    inp['peer_k1'] = normal((DEPTH, PEER_HEADS, N_KEYS, PEER_KEY_DIM // 2), (PEER_KEY_DIM // 2) ** -0.5)
    inp['peer_k2'] = normal((DEPTH, PEER_HEADS, N_KEYS, PEER_KEY_DIM // 2), (PEER_KEY_DIM // 2) ** -0.5)
    inp['peer_u'] = normal((DEPTH, N_EXPERTS, d), d ** -0.5)
    inp['peer_v'] = normal((DEPTH, N_EXPERTS, d), PEER_HEADS ** -0.5)
    return inp


def reference(x, c, ctx, c_ctx, mod_w, mod_b, norm_mix, norm_ffn, norm_final,
              attn_w_in, attn_q_gain, attn_k_gain, attn_w_out,
              ssm_w_in, ssm_lam_re, ssm_lam_im, ssm_log_dt, ssm_b_re, ssm_b_im,
              ssm_c_re, ssm_c_im, ssm_d, ssm_w_glu, ssm_w_out,
              lru_w_in, lru_conv_w, lru_conv_b, lru_w_rg, lru_b_rg, lru_w_ig, lru_b_ig,
              lru_lam, lru_w_out,
              peer_w_q, peer_k1, peer_k2, peer_u, peer_v):
    cos, sin = axial_rope_tables(x.shape[1])
    silu_c = jax.nn.silu(c)
    silu_cc = jax.nn.silu(c_ctx)
    for i in range(DEPTH):
        ctx_out = i < DEPTH - 1
        kind = i % N_MIXERS
        j = i // N_MIXERS
        m_lat = jnp.split((silu_c @ mod_w[i] + mod_b[i])[:, None, :], 6, axis=-1)
        m_ctx = jnp.split(silu_cc @ mod_w[i] + mod_b[i], 6, axis=-1)
        h_l = modulate(rmsnorm(x, norm_mix[i]), m_lat[0], m_lat[1])
        h_c = modulate(rmsnorm(ctx, norm_mix[i]), m_ctx[0], m_ctx[1])
        if kind == 0:
            y_c, y_l = attention_mixer(h_c, h_l, attn_w_in[j], attn_q_gain[j], attn_k_gain[j],
                                       attn_w_out[j], cos, sin, ctx_out)
        elif kind == 1:
            y_c, y_l = s5_mixer(h_c, h_l, ssm_w_in[j], ssm_lam_re[j], ssm_lam_im[j], ssm_log_dt[j],
                                ssm_b_re[j], ssm_b_im[j], ssm_c_re[j], ssm_c_im[j], ssm_d[j],
                                ssm_w_glu[j], ssm_w_out[j], ctx_out)
        else:
            y_c, y_l = rglru_mixer(h_c, h_l, lru_w_in[j], lru_conv_w[j], lru_conv_b[j], lru_w_rg[j],
                                   lru_b_rg[j], lru_w_ig[j], lru_b_ig[j], lru_lam[j], lru_w_out[j], ctx_out)
        x = x + m_lat[2] * y_l
        h_l = modulate(rmsnorm(x, norm_ffn[i]), m_lat[3], m_lat[4])
        x = x + m_lat[5] * peer_ffn(h_l, peer_w_q[i], peer_k1[i], peer_k2[i], peer_u[i], peer_v[i])
        if ctx_out:
            ctx = ctx + m_ctx[2] * y_c
            h_c = modulate(rmsnorm(ctx, norm_ffn[i]), m_ctx[3], m_ctx[4])
            ctx = ctx + m_ctx[5] * peer_ffn(h_c, peer_w_q[i], peer_k1[i], peer_k2[i], peer_u[i], peer_v[i])
    return rmsnorm(x, norm_final)
```

```python
import functools
import math

import numpy as np
import jax
import jax.numpy as jnp
from jax import lax
from jax.experimental import pallas as pl
from jax.experimental.pallas import tpu as pltpu

F32 = jnp.float32
BF16 = jnp.bfloat16

EPS = 1e-6
GRID_W = 64
ROPE_THETA = 10000.0
ATT_HEADS = 8
ATT_KV_HEADS = 2
ATT_GROUP = ATT_HEADS // ATT_KV_HEADS
SSM_GROUP = 16
LRU_BLOCKS = 8
CONV_W = 4
CONV_LEFT = 2
LRU_C = 8.0
PEER_HEADS = 8
PEER_TOPK = 16
N_MIXERS = 3

LANES = 128
SUBLANES = 8
VMEM_LIMIT = 48 * 1024 * 1024

ROW_TILE = 512
SEQ_TILE = 256


def _cparams(sem):
    return pltpu.CompilerParams(dimension_semantics=sem, vmem_limit_bytes=VMEM_LIMIT)


def _gelu(x):
    return 0.5 * x * (1.0 + lax.erf(x * (1.0 / math.sqrt(2.0))))


def _modnorm(x, g, shift, scale):
    y = x * lax.rsqrt(jnp.mean(x * x, axis=-1, keepdims=True) + EPS) * g
    return y * (1.0 + scale) + shift


def _mod_row(m_ref, sel, k, d):
    return m_ref[pl.ds(sel, 1), k * d:(k + 1) * d]


def _row_sel(i, tiles_per_batch, n_batch):
    return jnp.minimum(i // tiles_per_batch, n_batch)


def _mod_kernel(c_ref, w_ref, b_ref, o_ref):
    c = c_ref[...]
    s = c * jax.nn.sigmoid(c)
    o_ref[0] = jnp.dot(s.astype(BF16), w_ref[0].astype(BF16), preferred_element_type=F32) + b_ref[0]


def _modulation(crows, mod_w, mod_b):
    depth, d, n = mod_w.shape
    tn = 1536
    return pl.pallas_call(
        _mod_kernel,
        out_shape=jax.ShapeDtypeStruct((depth, SUBLANES, n), F32),
        grid=(depth, n // tn),
        in_specs=[pl.BlockSpec((SUBLANES, d), lambda l, j: (0, 0)),
                  pl.BlockSpec((1, d, tn), lambda l, j: (l, 0, j)),
                  pl.BlockSpec((1, 1, tn), lambda l, j: (l, 0, j))],
        out_specs=pl.BlockSpec((1, SUBLANES, tn), lambda l, j: (l, 0, j)),
        compiler_params=_cparams(("parallel", "parallel")),
        name="modulation",
    )(crows, mod_w, mod_b.reshape(depth, 1, n))


def _norm_matmul_kernel(x_ref, m_ref, g_ref, w_ref, o_ref, *, tpb, nb, k_shift):
    d = x_ref.shape[1]
    sel = _row_sel(pl.program_id(0), tpb, nb)
    h = _modnorm(x_ref[...], g_ref[...], _mod_row(m_ref, sel, k_shift, d), _mod_row(m_ref, sel, k_shift + 1, d))
    o_ref[...] = jnp.dot(h.astype(BF16), w_ref[...], preferred_element_type=F32).astype(o_ref.dtype)


def _norm_matmul(xa, m, g, w, *, n_tiles, tpb, nb, k_shift, out_dtype=F32):
    r, d = xa.shape
    n = w.shape[1]
    return pl.pallas_call(
        functools.partial(_norm_matmul_kernel, tpb=tpb, nb=nb, k_shift=k_shift),
        out_shape=jax.ShapeDtypeStruct((r, n), out_dtype),
        grid=(n_tiles,),
        in_specs=[pl.BlockSpec((ROW_TILE, d), lambda i: (i, 0)),
                  pl.BlockSpec(m.shape, lambda i: (0, 0)),
                  pl.BlockSpec((1, d), lambda i: (0, 0)),
                  pl.BlockSpec(w.shape, lambda i: (0, 0))],
        out_specs=pl.BlockSpec((ROW_TILE, n), lambda i: (i, 0)),
        compiler_params=_cparams(("parallel",)),
        name="norm_matmul",
    )(xa, m, g, w)


def _resid_matmul_kernel(x_ref, a_ref, w_ref, m_ref, o_ref, *, tpb, nb, k_gate):
    d = x_ref.shape[1]
    sel = _row_sel(pl.program_id(0), tpb, nb)
    y = jnp.dot(a_ref[...], w_ref[...], preferred_element_type=F32)
    o_ref[...] = x_ref[...] + _mod_row(m_ref, sel, k_gate, d) * y


def _resid_matmul(xa, a, w, m, *, n_tiles, tpb, nb, k_gate):
    r, d = xa.shape
    return pl.pallas_call(
        functools.partial(_resid_matmul_kernel, tpb=tpb, nb=nb, k_gate=k_gate),
        out_shape=jax.ShapeDtypeStruct((r, d), F32),
        grid=(n_tiles,),
        in_specs=[pl.BlockSpec((ROW_TILE, d), lambda i: (i, 0)),
                  pl.BlockSpec((ROW_TILE, a.shape[1]), lambda i: (i, 0)),
                  pl.BlockSpec(w.shape, lambda i: (0, 0)),
                  pl.BlockSpec(m.shape, lambda i: (0, 0))],
        out_specs=pl.BlockSpec((ROW_TILE, d), lambda i: (i, 0)),
        input_output_aliases={0: 0},
        compiler_params=_cparams(("parallel",)),
        name="resid_matmul",
    )(xa, a, w, m)


def _attn_qkv_kernel(x_ref, m_ref, g_ref, w_ref, qg_ref, kg_ref, cos_ref, sin_ref,
                     q_ref, k_ref, v_ref, *, tpb, nb, qscale):
    d = x_ref.shape[1]
    hd = qg_ref.shape[1]
    sel = _row_sel(pl.program_id(0), tpb, nb)
    h = _modnorm(x_ref[...], g_ref[...], _mod_row(m_ref, sel, 0, d), _mod_row(m_ref, sel, 1, d))
    qkv = jnp.dot(h.astype(BF16), w_ref[...], preferred_element_type=F32)
    cosf = cos_ref[...]
    sinf = sin_ref[...]

    def norm_rope(z, gain):
        zn = z * lax.rsqrt(jnp.mean(z * z, axis=-1, keepdims=True) + EPS) * gain
        return zn * cosf + pltpu.roll(zn, hd // 2, 1) * sinf

    nq = q_ref.shape[1] // hd
    nk = k_ref.shape[1] // hd
    for j in range(nq):
        q_ref[:, j * hd:(j + 1) * hd] = (norm_rope(qkv[:, j * hd:(j + 1) * hd], qg_ref[...]) * qscale).astype(BF16)
    for j in range(nk):
        c0 = (nq + j) * hd
        k_ref[:, j * hd:(j + 1) * hd] = norm_rope(qkv[:, c0:c0 + hd], kg_ref[...]).astype(BF16)
    v_ref[...] = qkv[:, (nq + nk) * hd:].astype(BF16)


def _attn_qkv(xa, m, g, w, qg, kg, cosf, sinf, *, n_tiles, tpb, nb, qscale):
    r, d = xa.shape
    hd = qg.shape[1]
    nq, nk = ATT_HEADS * hd, ATT_KV_HEADS * hd
    tab_map = lambda i: (jnp.where(i < nb * tpb, i % tpb, tpb), 0)
    return pl.pallas_call(
        functools.partial(_attn_qkv_kernel, tpb=tpb, nb=nb, qscale=qscale),
        out_shape=(jax.ShapeDtypeStruct((r, nq), BF16), jax.ShapeDtypeStruct((r, nk), BF16),
                   jax.ShapeDtypeStruct((r, nk), BF16)),
        grid=(n_tiles,),
        in_specs=[pl.BlockSpec((ROW_TILE, d), lambda i: (i, 0)),
                  pl.BlockSpec(m.shape, lambda i: (0, 0)),
                  pl.BlockSpec((1, d), lambda i: (0, 0)),
                  pl.BlockSpec(w.shape, lambda i: (0, 0)),
                  pl.BlockSpec((1, hd), lambda i: (0, 0)),
                  pl.BlockSpec((1, hd), lambda i: (0, 0)),
                  pl.BlockSpec((ROW_TILE, hd), tab_map),
                  pl.BlockSpec((ROW_TILE, hd), tab_map)],
        out_specs=(pl.BlockSpec((ROW_TILE, nq), lambda i: (i, 0)),
                   pl.BlockSpec((ROW_TILE, nk), lambda i: (i, 0)),
                   pl.BlockSpec((ROW_TILE, nk), lambda i: (i, 0))),
        compiler_params=_cparams(("parallel",)),
        name="attn_qkv",
    )(xa, m, g, w, qg, kg, cosf, sinf)


def _flash_kernel(*refs, tq, tk, hd, n_lat_chunks):
    if n_lat_chunks:
        q_ref, kc_ref, vc_ref, kl_ref, vl_ref, o_ref = refs
    else:
        q_ref, kc_ref, vc_ref, o_ref = refs
    qs = jnp.concatenate([q_ref[:, g * hd:(g + 1) * hd] for g in range(ATT_GROUP)], axis=0)
    rows = ATT_GROUP * tq

    def step(kblk, vblk, carry):
        m, l, acc = carry
        s = lax.dot_general(qs, kblk, (((1,), (1,)), ((), ())), preferred_element_type=F32)
        m_new = jnp.maximum(m, jnp.max(s, axis=-1, keepdims=True))
        p = jnp.exp2(s - m_new)
        alpha = jnp.exp2(m - m_new)
        l = alpha * l + jnp.sum(p, axis=-1, keepdims=True)
        acc = alpha * acc + jnp.dot(p.astype(BF16), vblk, preferred_element_type=F32)
        return m_new, l, acc

    carry = (jnp.full((rows, 1), -jnp.inf, F32), jnp.zeros((rows, 1), F32), jnp.zeros((rows, hd), F32))
    carry = step(kc_ref[...], vc_ref[...], carry)
    if n_lat_chunks:
        def body(c, carry):
            off = pl.multiple_of(c * tk, tk)
            return step(kl_ref[pl.ds(off, tk), :], vl_ref[pl.ds(off, tk), :], carry)
        carry = lax.fori_loop(0, n_lat_chunks, body, carry)
    _, l, acc = carry
    o = acc / l
    for g in range(ATT_GROUP):
        o_ref[:, g * hd:(g + 1) * hd] = o[g * tq:(g + 1) * tq].astype(BF16)


def _flash(q, k, v, *, nb, t_lat, n_ctx, hd, lat_queries):
    gw = ATT_GROUP * hd
    ctx_base = nb * t_lat // n_ctx
    kv_ctx = pl.BlockSpec((n_ctx, hd), lambda b, h, i: (ctx_base + b, h))
    if lat_queries:
        tq = min(256, t_lat)
        tk = min(1024, t_lat)
        nq_tiles = t_lat // tq
        q_spec = pl.BlockSpec((tq, gw), lambda b, h, i: (b * nq_tiles + i, h))
        kv_lat = pl.BlockSpec((t_lat, hd), lambda b, h, i: (b, h))
        in_specs = [q_spec, kv_ctx, kv_ctx, kv_lat, kv_lat]
        args = (q, k, v, k, v)
        n_chunks = t_lat // tk
        out_rows = nb * t_lat
    else:
        tq = n_ctx
        tk = n_ctx
        nq_tiles = 1
        q_spec = pl.BlockSpec((tq, gw), lambda b, h, i: (ctx_base + b, h))
        in_specs = [q_spec, kv_ctx, kv_ctx]
        args = (q, k, v)
        n_chunks = 0
        out_rows = nb * n_ctx
    return pl.pallas_call(
        functools.partial(_flash_kernel, tq=tq, tk=tk, hd=hd, n_lat_chunks=n_chunks),
        out_shape=jax.ShapeDtypeStruct((out_rows, q.shape[1]), BF16),
        grid=(nb, ATT_KV_HEADS, nq_tiles),
        in_specs=in_specs,
        out_specs=pl.BlockSpec((tq, gw), lambda b, h, i: (b * nq_tiles + i, h)),
        compiler_params=_cparams(("parallel", "parallel", "parallel")),
        name="flash_lat" if lat_queries else "flash_ctx",
    )(*args)


def _rope_tables(t_lat, hd):
    rows = t_lat // GRID_W
    r_idx, c_idx = jnp.meshgrid(jnp.arange(rows, dtype=F32), jnp.arange(GRID_W, dtype=F32), indexing='ij')
    pairs = hd // 4
    freqs = ROPE_THETA ** (-jnp.arange(pairs, dtype=F32) / pairs)
    ang = jnp.concatenate([r_idx.reshape(-1, 1) * freqs, c_idx.reshape(-1, 1) * freqs], axis=-1)
    cos, sin = jnp.cos(ang), jnp.sin(ang)
    cosf = jnp.concatenate([cos, cos], axis=-1)
    sinf = jnp.concatenate([-sin, sin], axis=-1)
    cosf = jnp.concatenate([cosf, jnp.ones((ROW_TILE, hd), F32)], axis=0)
    sinf = jnp.concatenate([sinf, jnp.zeros((ROW_TILE, hd), F32)], axis=0)
    return cosf, sinf


def _deinterleave_perm(hd):
    return np.concatenate([np.arange(0, hd, 2), np.arange(1, hd, 2)])


def _attention_layer(xa, m, g, w_in, q_gain, k_gain, w_out, geo, ctx_out):
    nb, t_lat, n_ctx, tpb, all_tiles, n_tiles = geo
    hd = q_gain.shape[0]
    perm = _deinterleave_perm(hd)
    nqk = (ATT_HEADS + ATT_KV_HEADS) * hd
    col = np.concatenate([(np.arange(nqk) // hd) * hd + perm[np.arange(nqk) % hd],
                          np.arange(nqk, w_in.shape[1])])
    w = w_in[:, col].astype(BF16)
    cosf, sinf = _rope_tables(t_lat, hd)
    qscale = hd ** -0.5 * math.log2(math.e)
    q, k, v = _attn_qkv(xa, m, g, w, q_gain[perm].reshape(1, hd), k_gain[perm].reshape(1, hd), cosf, sinf,
                        n_tiles=all_tiles, tpb=tpb, nb=nb, qscale=qscale)
    o = _flash(q, k, v, nb=nb, t_lat=t_lat, n_ctx=n_ctx, hd=hd, lat_queries=True)
    if ctx_out:
        o_c = _flash(q, k, v, nb=nb, t_lat=t_lat, n_ctx=n_ctx, hd=hd, lat_queries=False)
        o = jnp.concatenate([o, o_c], axis=0)
    return _resid_matmul(xa, o, w_out.astype(BF16), m, n_tiles=n_tiles, tpb=tpb, nb=nb, k_gate=2)


def _seq_block(b, c, *, nb, nc_ctx, nc_lat, reverse):
    ctx_base = nb * nc_lat
    in_ctx = c < nc_ctx
    cl = c - nc_ctx
    if reverse:
        return jnp.where(in_ctx, ctx_base + b * nc_ctx + (nc_ctx - 1 - c), b * nc_lat + (nc_lat - 1 - cl))
    return jnp.where(in_ctx, ctx_base + b * nc_ctx + c, b * nc_lat + cl)


def _roll_rows(x, s, reverse):
    return pltpu.roll(x, SUBLANES - s if reverse else s, 0)


def _last_row(x, reverse):
    return jnp.broadcast_to(x[0:1, :] if reverse else x[SUBLANES - 1:SUBLANES, :], x.shape)


def _s5_scan_kernel(u_ref, bd_ref, tab_ref, cd_ref, y_ref, hs_ref, st_ref, *, tc, ns, reverse):
    c = pl.program_id(1)

    @pl.when(c == 0)
    def _():
        st_ref[...] = jnp.zeros_like(st_ref)

    hs_ref[...] = jnp.dot(u_ref[...].astype(BF16), bd_ref[0], preferred_element_type=F32)
    ng = tc // SUBLANES

    def body(gi, carry):
        hr, hi = carry
        r0 = pl.multiple_of(((ng - 1 - gi) if reverse else gi) * SUBLANES, SUBLANES)
        br = hs_ref[pl.ds(r0, SUBLANES), 0:ns]
        bi = hs_ref[pl.ds(r0, SUBLANES), ns:2 * ns]
        for k, s in enumerate((1, 2, 4)):
            rr, ri = _roll_rows(br, s, reverse), _roll_rows(bi, s, reverse)
            mr, mi = tab_ref[0, 2 * k], tab_ref[0, 2 * k + 1]
            br, bi = br + mr * rr - mi * ri, bi + mr * ri + mi * rr
        pr, pi_ = tab_ref[0, 6], tab_ref[0, 7]
        nr = br + pr * hr - pi_ * hi
        ni = bi + pr * hi + pi_ * hr
        hs_ref[pl.ds(r0, SUBLANES), 0:ns] = nr
        hs_ref[pl.ds(r0, SUBLANES), ns:2 * ns] = ni
        return _last_row(nr, reverse), _last_row(ni, reverse)

    hr, hi = lax.fori_loop(0, ng, body, (st_ref[:, 0:ns], st_ref[:, ns:2 * ns]))
    st_ref[:, 0:ns] = hr
    st_ref[:, ns:2 * ns] = hi
    y_ref[...] = jnp.dot(hs_ref[...].astype(BF16), cd_ref[0], preferred_element_type=F32)


def _s5_scan(u, bd, tab, cd, geo, direction):
    nb, t_lat, n_ctx = geo[:3]
    r, width = u.shape
    ns = tab.shape[3]
    tc = SEQ_TILE
    nc_ctx, nc_lat = n_ctx // tc, t_lat // tc
    blk = functools.partial(_seq_block, nb=nb, nc_ctx=nc_ctx, nc_lat=nc_lat, reverse=direction == 1)
    return pl.pallas_call(
        functools.partial(_s5_scan_kernel, tc=tc, ns=ns, reverse=direction == 1),
        out_shape=jax.ShapeDtypeStruct((r, width), F32),
        grid=(nb, nc_ctx + nc_lat),
        in_specs=[pl.BlockSpec((tc, width), lambda b, c: (blk(b, c), 0)),
                  pl.BlockSpec((1, width, 2 * ns), lambda b, c: (direction, 0, 0)),
                  pl.BlockSpec((1, 8, SUBLANES, ns), lambda b, c: (direction, 0, 0, 0)),
                  pl.BlockSpec((1, 2 * ns, width), lambda b, c: (direction, 0, 0))],
        out_specs=pl.BlockSpec((tc, width), lambda b, c: (blk(b, c), 0)),
        scratch_shapes=[pltpu.VMEM((tc, 2 * ns), F32), pltpu.VMEM((SUBLANES, 2 * ns), F32)],
        compiler_params=_cparams(("arbitrary", "arbitrary")),
        name="s5_scan_bwd" if direction else "s5_scan_fwd",
    )(u, bd, tab, cd)


def _s5_out_kernel(x_ref, u_ref, yf_ref, yb_ref, dsk_ref, wg_ref, wo_ref, m_ref, o_ref, *, tpb, nb):
    d = x_ref.shape[1]
    sel = _row_sel(pl.program_id(0), tpb, nb)
    y = dsk_ref[...] * u_ref[...] + yf_ref[...] + yb_ref[...]
    z = _gelu(y)
    z = z * jax.nn.sigmoid(jnp.dot(z.astype(BF16), wg_ref[...], preferred_element_type=F32))
    out = jnp.dot(z.astype(BF16), wo_ref[...], preferred_element_type=F32)
    o_ref[...] = x_ref[...] + _mod_row(m_ref, sel, 2, d) * out


def _s5_out(xa, u, yf, yb, dsk, wg, wo, m, *, n_tiles, tpb, nb):
    r, d = xa.shape
    width = u.shape[1]
    return pl.pallas_call(
        functools.partial(_s5_out_kernel, tpb=tpb, nb=nb),
        out_shape=jax.ShapeDtypeStruct((r, d), F32),
        grid=(n_tiles,),
        in_specs=[pl.BlockSpec((ROW_TILE, d), lambda i: (i, 0)),
                  pl.BlockSpec((ROW_TILE, width), lambda i: (i, 0)),
                  pl.BlockSpec((ROW_TILE, width), lambda i: (i, 0)),
                  pl.BlockSpec((ROW_TILE, width), lambda i: (i, 0)),
                  pl.BlockSpec((1, width), lambda i: (0, 0)),
                  pl.BlockSpec(wg.shape, lambda i: (0, 0)),
                  pl.BlockSpec(wo.shape, lambda i: (0, 0)),
                  pl.BlockSpec(m.shape, lambda i: (0, 0))],
        out_specs=pl.BlockSpec((ROW_TILE, d), lambda i: (i, 0)),
        input_output_aliases={0: 0},
        compiler_params=_cparams(("parallel",)),
        name="s5_out",
    )(xa, u, yf, yb, dsk, wg, wo, m)


def _s5_layer(xa, m, g, w_in, lam_re, lam_im, log_dt, b_re, b_im, c_re, c_im, d_skip, w_glu, w_out, geo):
    nb, t_lat, n_ctx, tpb, all_tiles, n_tiles = geo
    groups, p_state = lam_re.shape[1], lam_re.shape[2]
    width = w_in.shape[1]
    ns = groups * p_state
    lam = lax.complex(lam_re.astype(F32), lam_im.astype(F32))
    dt = jnp.exp(log_dt.astype(F32))[..., None]
    lam_bar = jnp.exp(lam * dt)
    b_bar = ((lam_bar - 1.0) / lam)[..., None] * lax.complex(b_re.astype(F32), b_im.astype(F32))
    eye = jnp.eye(groups, dtype=F32)

    def block_diag_in(z):
        return jnp.einsum('dgpc,gh->dgchp', z, eye).reshape(2, groups * SSM_GROUP, ns)

    def block_diag_out(z):
        return jnp.einsum('dgcp,gh->dgphc', z, eye).reshape(2, ns, groups * SSM_GROUP)

    bd = jnp.concatenate([block_diag_in(jnp.real(b_bar)), block_diag_in(jnp.imag(b_bar))], axis=-1).astype(BF16)
    cd = jnp.concatenate([block_diag_out(c_re.astype(F32)), block_diag_out(-c_im.astype(F32))], axis=1).astype(BF16)
    pw = [lam_bar.reshape(2, ns)]
    for _ in range(SUBLANES - 1):
        pw.append(pw[-1] * pw[0])
    row = jnp.arange(SUBLANES)
    tabs = []
    for dirn in range(2):
        rev = dirn == 1
        planes = []
        for s in (1, 2, 4):
            keep = (row + s <= SUBLANES - 1) if rev else (row >= s)
            ms = jnp.where(keep[:, None], pw[s - 1][dirn][None, :], 0.0)
            planes += [jnp.real(ms), jnp.imag(ms)]
        pows = jnp.stack([pw[(SUBLANES - 1 - t) if rev else t][dirn] for t in range(SUBLANES)], axis=0)
        planes += [jnp.real(pows), jnp.imag(pows)]
        tabs.append(jnp.stack(planes, axis=0))
    tab = jnp.stack(tabs, axis=0).astype(F32)

    u = _norm_matmul(xa, m, g, w_in.astype(BF16), n_tiles=all_tiles, tpb=tpb, nb=nb, k_shift=0)
    yf = _s5_scan(u, bd, tab, cd, geo, 0)
    yb = _s5_scan(u, bd, tab, cd, geo, 1)
    return _s5_out(xa, u, yf, yb, d_skip.reshape(1, width).astype(F32), w_glu.astype(BF16), w_out.astype(BF16), m,
                   n_tiles=n_tiles, tpb=tpb, nb=nb)


def _lru_coef_kernel(cur_ref, prev_ref, next_ref, cw_ref, cb_ref, wrg_ref, brg_ref, wig_ref, big_ref, cn_ref,
                     a_ref, b_ref, buf_ref, *, tm, tps_lat, tps_ctx, n_lat_tiles):
    i = pl.program_id(0)
    j = i - n_lat_tiles
    first = jnp.where(i < n_lat_tiles, i % tps_lat == 0, j % tps_ctx == 0)
    last = jnp.where(i < n_lat_tiles, i % tps_lat == tps_lat - 1, j % tps_ctx == tps_ctx - 1)
    h = SUBLANES
    buf_ref[0:h, :] = jnp.where(first, 0.0, prev_ref[...])
    buf_ref[h:h + tm, :] = cur_ref[...]
    buf_ref[h + tm:h + tm + h, :] = jnp.where(last, 0.0, next_ref[...])
    xr = cb_ref[...] + sum(cw_ref[k:k + 1, :] * buf_ref[h - CONV_LEFT + k:h - CONV_LEFT + k + tm, :]
                           for k in range(CONV_W))
    nblk = wrg_ref.shape[1]
    bw = wrg_ref.shape[2]
    for n in range(nblk):
        sl = slice(n * bw, (n + 1) * bw)
        xs = xr[:, sl]
        xb = xs.astype(BF16)
        for d in range(2):
            r = jax.nn.sigmoid(jnp.dot(xb, wrg_ref[d, n], preferred_element_type=F32) + brg_ref[d, :, sl])
            ig = jax.nn.sigmoid(jnp.dot(xb, wig_ref[d, n], preferred_element_type=F32) + big_ref[d, :, sl])
            log_a = cn_ref[d, :, sl] * r
            a_ref[d, :, sl] = jnp.exp(log_a)
            b_ref[d, :, sl] = jnp.sqrt(1.0 - jnp.exp(2.0 * log_a)) * (ig * xs)


def _lru_coef(gx, conv_w, conv_b, wrg, brg, wig, big, cneg, geo):
    nb, t_lat, n_ctx = geo[:3]
    r = gx.shape[0]
    width = conv_w.shape[1]
    tm = SEQ_TILE
    hb = tm // SUBLANES
    n_lat_tiles = nb * t_lat // tm
    last_halo = r // SUBLANES - 1
    vec = lambda a: pl.BlockSpec(a.shape, lambda i: (0,) * a.ndim)
    return pl.pallas_call(
        functools.partial(_lru_coef_kernel, tm=tm, tps_lat=t_lat // tm, tps_ctx=n_ctx // tm, n_lat_tiles=n_lat_tiles),
        out_shape=(jax.ShapeDtypeStruct((2, r, width), F32), jax.ShapeDtypeStruct((2, r, width), F32)),
        grid=(r // tm,),
        in_specs=[pl.BlockSpec((tm, width), lambda i: (i, 1)),
                  pl.BlockSpec((SUBLANES, width), lambda i: (jnp.maximum(i * hb - 1, 0), 1)),
                  pl.BlockSpec((SUBLANES, width), lambda i: (jnp.minimum((i + 1) * hb, last_halo), 1)),
                  vec(conv_w), vec(conv_b), vec(wrg), vec(brg), vec(wig), vec(big), vec(cneg)],
        out_specs=(pl.BlockSpec((2, tm, width), lambda i: (0, i, 0)),
                   pl.BlockSpec((2, tm, width), lambda i: (0, i, 0))),
        scratch_shapes=[pltpu.VMEM((tm + 2 * SUBLANES, width), F32)],
        compiler_params=_cparams(("parallel",)),
        name="lru_coef",
    )(gx, gx, gx, conv_w, conv_b, wrg, brg, wig, big, cneg)


def _lru_scan_kernel(a_ref, b_ref, o_ref, st_ref, *, tc, reverse):
    c = pl.program_id(1)

    @pl.when(c == 0)
    def _():
        st_ref[...] = jnp.zeros_like(st_ref)

    width = a_ref.shape[-1]
    row = lax.broadcasted_iota(jnp.int32, (SUBLANES, width), 0)
    ng = tc // SUBLANES

    def body(gi, h):
        r0 = pl.multiple_of(((ng - 1 - gi) if reverse else gi) * SUBLANES, SUBLANES)
        a = a_ref[0, pl.ds(r0, SUBLANES), :]
        b = b_ref[0, pl.ds(r0, SUBLANES), :]
        for s in (1, 2, 4):
            keep = (row < SUBLANES - s) if reverse else (row >= s)
            ar = jnp.where(keep, _roll_rows(a, s, reverse), 1.0)
            br = jnp.where(keep, _roll_rows(b, s, reverse), 0.0)
            b = a * br + b
            a = a * ar
        hn = a * h + b
        o_ref[pl.ds(r0, SUBLANES), :] = hn
        return _last_row(hn, reverse)

    st_ref[...] = lax.fori_loop(0, ng, body, st_ref[...])


def _lru_scan(a, b, geo, direction):
    nb, t_lat, n_ctx = geo[:3]
    _, r, width = a.shape
    tc = SEQ_TILE
    nc_ctx, nc_lat = n_ctx // tc, t_lat // tc
    blk = functools.partial(_seq_block, nb=nb, nc_ctx=nc_ctx, nc_lat=nc_lat, reverse=direction == 1)
    spec = pl.BlockSpec((1, tc, width), lambda b_, c: (direction, blk(b_, c), 0))
    return pl.pallas_call(
        functools.partial(_lru_scan_kernel, tc=tc, reverse=direction == 1),
        out_shape=jax.ShapeDtypeStruct((r, width), F32),
        grid=(nb, nc_ctx + nc_lat),
        in_specs=[spec, spec],
        out_specs=pl.BlockSpec((tc, width), lambda b_, c: (blk(b_, c), 0)),
        scratch_shapes=[pltpu.VMEM((SUBLANES, width), F32)],
        compiler_params=_cparams(("arbitrary", "arbitrary")),
        name="lru_scan_bwd" if direction else "lru_scan_fwd",
    )(a, b)


def _lru_out_kernel(x_ref, gate_ref, sf_ref, sb_ref, wo_ref, m_ref, o_ref, *, tpb, nb):
    d = x_ref.shape[1]
    sel = _row_sel(pl.program_id(0), tpb, nb)
    y = (sf_ref[...] + sb_ref[...]) * _gelu(gate_ref[...])
    out = jnp.dot(y.astype(BF16), wo_ref[...], preferred_element_type=F32)
    o_ref[...] = x_ref[...] + _mod_row(m_ref, sel, 2, d) * out


def _lru_out(xa, gx, sf, sb, wo, m, *, n_tiles, tpb, nb):
    r, d = xa.shape
    width = sf.shape[1]
    return pl.pallas_call(
        functools.partial(_lru_out_kernel, tpb=tpb, nb=nb),
        out_shape=jax.ShapeDtypeStruct((r, d), F32),
        grid=(n_tiles,),
        in_specs=[pl.BlockSpec((ROW_TILE, d), lambda i: (i, 0)),
                  pl.BlockSpec((ROW_TILE, width), lambda i: (i, 0)),
                  pl.BlockSpec((ROW_TILE, width), lambda i: (i, 0)),
                  pl.BlockSpec((ROW_TILE, width), lambda i: (i, 0)),
                  pl.BlockSpec(wo.shape, lambda i: (0, 0)),
                  pl.BlockSpec(m.shape, lambda i: (0, 0))],
        out_specs=pl.BlockSpec((ROW_TILE, d), lambda i: (i, 0)),
        input_output_aliases={0: 0},
        compiler_params=_cparams(("parallel",)),
        name="lru_out",
    )(xa, gx, sf, sb, wo, m)


def _lru_layer(xa, m, g, w_in, conv_w, conv_b, w_rg, b_rg, w_ig, b_ig, lam, w_out, geo):
    nb, t_lat, n_ctx, tpb, all_tiles, n_tiles = geo
    width = conv_w.shape[1]
    gx = _norm_matmul(xa, m, g, w_in.astype(BF16), n_tiles=all_tiles, tpb=tpb, nb=nb, k_shift=0)
    cneg = (-LRU_C * jax.nn.softplus(-lam.astype(F32))).reshape(2, 1, width)
    a, b = _lru_coef(gx, conv_w.astype(F32), conv_b.reshape(1, width).astype(F32),
                     w_rg.astype(BF16), b_rg.reshape(2, 1, width).astype(F32),
                     w_ig.astype(BF16), b_ig.reshape(2, 1, width).astype(F32), cneg, geo)
    sf = _lru_scan(a, b, geo, 0)
    sb = _lru_scan(a, b, geo, 1)
    return _lru_out(xa, gx, sf, sb, w_out.astype(BF16), m, n_tiles=n_tiles, tpb=tpb, nb=nb)


def _oddeven_merge_sort_pairs(n):
    pairs = []
    p = 1
    while p < n:
        k = p
        while k >= 1:
            for j in range(k % p, n - k, 2 * k):
                for i in range(min(k, n - j - k)):
                    if (i + j) // (2 * p) == (i + j + k) // (2 * p):
                        pairs.append((i + j, i + j + k))
            k //= 2
        p *= 2
    return pairs


_SORT16 = _oddeven_merge_sort_pairs(PEER_TOPK)


def _ce(v, i, j):
    a, b = v[i], v[j]
    v[i] = jnp.maximum(a, b)
    v[j] = jnp.minimum(a, b)


def _bitonic_merge_desc(v):
    n = len(v)
    k = n // 2
    while k >= 1:
        for i in range(n):
            if i & k == 0:
                _ce(v, i, i | k)
        k //= 2


def _top16_desc(blk):
    n = PEER_TOPK
    v = [blk[SUBLANES * a:SUBLANES * (a + 1), :] for a in range(n)]
    for (i, j) in _SORT16:
        _ce(v, i, j)
    for shift in (4, 2, 1):
        o = [pltpu.roll(x, shift, 0) for x in v]
        v = [jnp.maximum(v[i], o[n - 1 - i]) for i in range(n)]
        _bitonic_merge_desc(v)
    return v


def _peer_score_kernel(x_ref, m_ref, g_ref, wq_ref, kk_ref, h_ref, st_ref, stat_ref, *, tpb, nb):
    d = x_ref.shape[1]
    tm = x_ref.shape[0]
    nk = kk_ref.shape[1]
    kd = kk_ref.shape[2]
    nh = PEER_HEADS
    sel = _row_sel(pl.program_id(0), tpb, nb)
    h = _modnorm(x_ref[...], g_ref[...], _mod_row(m_ref, sel, 3, d), _mod_row(m_ref, sel, 4, d)).astype(BF16)
    h_ref[...] = h
    q = jnp.dot(h, wq_ref[...], preferred_element_type=F32).astype(BF16)

    sub = lax.broadcasted_iota(jnp.int32, (SUBLANES, tm), 0)
    packed = []
    for half in range(2):
        pk = [jnp.zeros((SUBLANES, tm), F32) for _ in range(PEER_TOPK)]
        for hh in range(nh):
            jq = hh * 2 + half
            s = lax.dot_general(kk_ref[half * nh + hh], q[:, jq * kd:(jq + 1) * kd],
                                (((1,), (1,)), ((), ())), preferred_element_type=F32)
            st_ref[(half * nh + hh) * nk:(half * nh + hh + 1) * nk, :] = s
            top = _top16_desc(s)
            pk = [jnp.where(sub == hh, top[i], pk[i]) for i in range(PEER_TOPK)]
        packed.append(pk)
    av, bv = packed
    m1, m2 = av[0], bv[0]
    cand = [av[i] + bv[j] for i in range(PEER_TOPK) for j in range(PEER_TOPK) if (i + 1) * (j + 1) <= PEER_TOPK]
    cur = list(cand)
    for _ in range(PEER_TOPK - 1):
        mx = functools.reduce(jnp.maximum, cur)
        cur = [jnp.where(c == mx, -jnp.inf, c) for c in cur]
    tau = functools.reduce(jnp.maximum, cur)
    ea = [jnp.exp(a - m1) for a in av]
    eb = [jnp.exp(b - m2) for b in bv]
    z = jnp.zeros((SUBLANES, tm), F32)
    idx = 0
    for i in range(PEER_TOPK):
        for j in range(PEER_TOPK):
            if (i + 1) * (j + 1) <= PEER_TOPK:
                z = z + jnp.where(cand[idx] >= tau, ea[i] * eb[j], 0.0)
                idx += 1
    stat_ref[0 * nh:1 * nh, :] = tau
    stat_ref[1 * nh:2 * nh, :] = m1
    stat_ref[2 * nh:3 * nh, :] = m2
    stat_ref[3 * nh:4 * nh, :] = 1.0 / z


def _peer_scores(xa, m, g, wq, kk, *, n_tiles, tpb, nb):
    r, d = xa.shape
    nk = kk.shape[1]
    return pl.pallas_call(
        functools.partial(_peer_score_kernel, tpb=tpb, nb=nb),
        out_shape=(jax.ShapeDtypeStruct((r, d), BF16),
                   jax.ShapeDtypeStruct((2 * PEER_HEADS * nk, r), F32),
                   jax.ShapeDtypeStruct((4 * PEER_HEADS, r), F32)),
        grid=(n_tiles,),
        in_specs=[pl.BlockSpec((ROW_TILE, d), lambda i: (i, 0)),
                  pl.BlockSpec(m.shape, lambda i: (0, 0)),
                  pl.BlockSpec((1, d), lambda i: (0, 0)),
                  pl.BlockSpec(wq.shape, lambda i: (0, 0)),
                  pl.BlockSpec(kk.shape, lambda i: (0, 0, 0))],
        out_specs=(pl.BlockSpec((ROW_TILE, d), lambda i: (i, 0)),
                   pl.BlockSpec((2 * PEER_HEADS * nk, ROW_TILE), lambda i: (0, i)),
                   pl.BlockSpec((4 * PEER_HEADS, ROW_TILE), lambda i: (0, i))),
        compiler_params=_cparams(("parallel",)),
        name="peer_scores",
    )(xa, m, g, wq, kk)


def _peer_dense_kernel(x_ref, h_ref, st_ref, stat_ref, u_ref, vt_ref, m_ref, o_ref,
                       acc_ref, aex_ref, bex_ref, s_ref, p_ref, *, tpb, nb, nk):
    d = x_ref.shape[1]
    tm = x_ref.shape[0]
    eb = u_ref.shape[0]
    nh = PEER_HEADS
    j = pl.program_id(1)

    @pl.when(j == 0)
    def _():
        acc_ref[...] = jnp.zeros_like(acc_ref)
        for hh in range(nh):
            m1 = stat_ref[nh + hh:nh + hh + 1, :]
            m2 = stat_ref[2 * nh + hh:2 * nh + hh + 1, :]
            zi = stat_ref[3 * nh + hh:3 * nh + hh + 1, :]
            aex_ref[hh * nk:(hh + 1) * nk, :] = jnp.exp(st_ref[hh * nk:(hh + 1) * nk, :] - m1) * zi
            bex_ref[hh * nk:(hh + 1) * nk, :] = jnp.exp(st_ref[(nh + hh) * nk:(nh + hh + 1) * nk, :] - m2)

    s_ref[...] = lax.dot_general(u_ref[...], h_ref[...], (((1,), (1,)), ((), ())), preferred_element_type=F32)

    n_i1 = eb // nk
    assert n_i1 == SUBLANES
    for tl in range(tm // LANES):
        ls = slice(tl * LANES, (tl + 1) * LANES)
        for ii in range(n_i1):
            w = jnp.zeros((nk, LANES), F32)
            for hh in range(nh):
                base = pl.multiple_of(hh * nk + j * n_i1, SUBLANES)
                s1 = st_ref[pl.ds(base, SUBLANES), ls][ii:ii + 1, :]
                a1 = aex_ref[pl.ds(base, SUBLANES), ls][ii:ii + 1, :]
                s2 = st_ref[(nh + hh) * nk:(nh + hh + 1) * nk, ls]
                tau = stat_ref[hh:hh + 1, ls]
                w = w + jnp.where(s2 + s1 >= tau, bex_ref[hh * nk:(hh + 1) * nk, ls], 0.0) * a1
            rs = slice(ii * nk, (ii + 1) * nk)
            p_ref[rs, ls] = (w * _gelu(s_ref[rs, ls])).astype(BF16)
    acc_ref[...] += jnp.dot(vt_ref[...], p_ref[...], preferred_element_type=F32)

    @pl.when(j == pl.num_programs(1) - 1)
    def _():
        sel = _row_sel(pl.program_id(0), tpb, nb)
        o_ref[...] = x_ref[...] + _mod_row(m_ref, sel, 5, d) * acc_ref[...].T


def _peer_dense(xa, h, st, stats, u, vt, m, *, n_tiles, tpb, nb, nk):
    r, d = xa.shape
    n_exp = u.shape[0]
    eb = 1024
    tm = ROW_TILE
    return pl.pallas_call(
        functools.partial(_peer_dense_kernel, tpb=tpb, nb=nb, nk=nk),
        out_shape=jax.ShapeDtypeStruct((r, d), F32),
        grid=(n_tiles, n_exp // eb),
        in_specs=[pl.BlockSpec((tm, d), lambda i, j: (i, 0)),
                  pl.BlockSpec((tm, d), lambda i, j: (i, 0)),
                  pl.BlockSpec((st.shape[0], tm), lambda i, j: (0, i)),
                  pl.BlockSpec((stats.shape[0], tm), lambda i, j: (0, i)),
                  pl.BlockSpec((eb, d), lambda i, j: (j, 0)),
                  pl.BlockSpec((d, eb), lambda i, j: (0, j)),
                  pl.BlockSpec(m.shape, lambda i, j: (0, 0))],
        out_specs=pl.BlockSpec((tm, d), lambda i, j: (i, 0)),
        scratch_shapes=[pltpu.VMEM((d, tm), F32),
                        pltpu.VMEM((PEER_HEADS * nk, tm), F32),
                        pltpu.VMEM((PEER_HEADS * nk, tm), F32),
                        pltpu.VMEM((eb, tm), F32),
                        pltpu.VMEM((eb, tm), BF16)],
        input_output_aliases={0: 0},
        compiler_params=_cparams(("parallel", "arbitrary")),
        name="peer_dense",
    )(xa, h, st, stats, u, vt, m)


def _peer_layer(xa, m, g, w_q, k1, k2, u_tab, v_tab, geo):
    nb, t_lat, n_ctx, tpb, all_tiles, n_tiles = geo
    nk = k1.shape[1]
    kk = jnp.concatenate([k1, k2], axis=0).astype(BF16)
    h, st, stats = _peer_scores(xa, m, g, w_q.astype(BF16), kk, n_tiles=n_tiles, tpb=tpb, nb=nb)
    return _peer_dense(xa, h, st, stats, u_tab.astype(BF16), v_tab.astype(BF16).T, m,
                       n_tiles=n_tiles, tpb=tpb, nb=nb, nk=nk)


def _final_norm_kernel(x_ref, g_ref, o_ref):
    x = x_ref[...]
    o_ref[...] = x * lax.rsqrt(jnp.mean(x * x, axis=-1, keepdims=True) + EPS) * g_ref[...]


def _final_norm(xa, g, n_rows):
    d = xa.shape[1]
    return pl.pallas_call(
        _final_norm_kernel,
        out_shape=jax.ShapeDtypeStruct((n_rows, d), F32),
        grid=(n_rows // ROW_TILE,),
        in_specs=[pl.BlockSpec((ROW_TILE, d), lambda i: (i, 0)), pl.BlockSpec((1, d), lambda i: (0, 0))],
        out_specs=pl.BlockSpec((ROW_TILE, d), lambda i: (i, 0)),
        compiler_params=_cparams(("parallel",)),
        name="final_norm",
    )(xa, g)


def kernel(x, c, ctx, c_ctx, mod_w, mod_b, norm_mix, norm_ffn, norm_final, attn_w_in, attn_q_gain, attn_k_gain, attn_w_out, ssm_w_in, ssm_lam_re, ssm_lam_im, ssm_log_dt, ssm_b_re, ssm_b_im, ssm_c_re, ssm_c_im, ssm_d, ssm_w_glu, ssm_w_out, lru_w_in, lru_conv_w, lru_conv_b, lru_w_rg, lru_b_rg, lru_w_ig, lru_b_ig, lru_lam, lru_w_out, peer_w_q, peer_k1, peer_k2, peer_u, peer_v):
    nb, t_lat, d = x.shape
    n_ctx = ctx.shape[1]
    depth = mod_w.shape[0]
    assert t_lat % ROW_TILE == 0 and (nb * n_ctx) % ROW_TILE == 0 and n_ctx % SEQ_TILE == 0
    assert nb + 1 <= SUBLANES and t_lat % GRID_W == 0
    tpb = t_lat // ROW_TILE
    lat_tiles = nb * tpb
    all_tiles = lat_tiles + nb * n_ctx // ROW_TILE

    xa = jnp.concatenate([x.reshape(nb * t_lat, d), ctx.reshape(nb * n_ctx, d)], axis=0).astype(F32)
    crows = jnp.concatenate([c.astype(F32), c_ctx.reshape(1, d).astype(F32),
                             jnp.zeros((SUBLANES - nb - 1, d), F32)], axis=0)
    mods = _modulation(crows, mod_w.astype(F32), mod_b.astype(F32))

    for i in range(depth):
        ctx_out = i < depth - 1
        geo = (nb, t_lat, n_ctx, tpb, all_tiles, all_tiles if ctx_out else lat_tiles)
        kind, j = i % N_MIXERS, i // N_MIXERS
        m = mods[i]
        g_mix = norm_mix[i].reshape(1, d).astype(F32)
        if kind == 0:
            xa = _attention_layer(xa, m, g_mix, attn_w_in[j], attn_q_gain[j], attn_k_gain[j], attn_w_out[j],
                                  geo, ctx_out)
        elif kind == 1:
            xa = _s5_layer(xa, m, g_mix, ssm_w_in[j], ssm_lam_re[j], ssm_lam_im[j], ssm_log_dt[j],
                           ssm_b_re[j], ssm_b_im[j], ssm_c_re[j], ssm_c_im[j], ssm_d[j],
                           ssm_w_glu[j], ssm_w_out[j], geo)
        else:
            xa = _lru_layer(xa, m, g_mix, lru_w_in[j], lru_conv_w[j], lru_conv_b[j], lru_w_rg[j], lru_b_rg[j],
                            lru_w_ig[j], lru_b_ig[j], lru_lam[j], lru_w_out[j], geo)
        xa = _peer_layer(xa, m, norm_ffn[i].reshape(1, d).astype(F32), peer_w_q[i], peer_k1[i], peer_k2[i],
                         peer_u[i], peer_v[i], geo)
    out = _final_norm(xa, norm_final.reshape(1, d).astype(F32), nb * t_lat)
    return out.reshape(nb, t_lat, d)
```

```python
import functools
import math

import numpy as np
import jax
import jax.numpy as jnp
from jax import lax
from jax.experimental import pallas as pl
from jax.experimental.pallas import tpu as pltpu

F32 = jnp.float32
BF16 = jnp.bfloat16

EPS = 1e-6
GRID_W = 64
ROPE_THETA = 10000.0
ATT_HEADS = 8
ATT_KV_HEADS = 2
ATT_GROUP = ATT_HEADS // ATT_KV_HEADS
SSM_GROUP = 16
LRU_BLOCKS = 8
CONV_W = 4
CONV_LEFT = 2
LRU_C = 8.0
PEER_HEADS = 8
PEER_TOPK = 16
N_MIXERS = 3

LANES = 128
SUBLANES = 8
VMEM_LIMIT = 48 * 1024 * 1024
PEER_VMEM_LIMIT = 56 * 1024 * 1024

ROW_TILE = 512
SEQ_TILE = 256


def _cparams(sem):
    return pltpu.CompilerParams(dimension_semantics=sem, vmem_limit_bytes=VMEM_LIMIT)


def _gelu(x):
    return 0.5 * x * (1.0 + lax.erf(x * (1.0 / math.sqrt(2.0))))


def _modnorm(x, g, shift, scale):
    y = x * lax.rsqrt(jnp.mean(x * x, axis=-1, keepdims=True) + EPS) * g
    return y * (1.0 + scale) + shift


def _mod_row(m_ref, sel, k, d):
    return m_ref[pl.ds(sel, 1), k * d:(k + 1) * d]


def _row_sel(i, tiles_per_batch, n_batch):
    return jnp.minimum(i // tiles_per_batch, n_batch)


def _mod_kernel(c_ref, w_ref, b_ref, o_ref):
    c = c_ref[...]
    s = c * jax.nn.sigmoid(c)
    o_ref[0] = jnp.dot(s.astype(BF16), w_ref[0].astype(BF16), preferred_element_type=F32) + b_ref[0]


def _modulation(crows, mod_w, mod_b):
    depth, d, n = mod_w.shape
    tn = 1536
    return pl.pallas_call(
        _mod_kernel,
        out_shape=jax.ShapeDtypeStruct((depth, SUBLANES, n), F32),
        grid=(depth, n // tn),
        in_specs=[pl.BlockSpec((SUBLANES, d), lambda l, j: (0, 0)),
                  pl.BlockSpec((1, d, tn), lambda l, j: (l, 0, j)),
                  pl.BlockSpec((1, 1, tn), lambda l, j: (l, 0, j))],
        out_specs=pl.BlockSpec((1, SUBLANES, tn), lambda l, j: (l, 0, j)),
        compiler_params=_cparams(("parallel", "parallel")),
        name="modulation",
    )(crows, mod_w, mod_b.reshape(depth, 1, n))


def _norm_matmul_kernel(x_ref, m_ref, g_ref, w_ref, o_ref, *, tpb, nb, k_shift):
    d = x_ref.shape[1]
    sel = _row_sel(pl.program_id(0), tpb, nb)
    h = _modnorm(x_ref[...], g_ref[...], _mod_row(m_ref, sel, k_shift, d), _mod_row(m_ref, sel, k_shift + 1, d))
    o_ref[...] = jnp.dot(h.astype(BF16), w_ref[...], preferred_element_type=F32).astype(o_ref.dtype)


def _norm_matmul(xa, m, g, w, *, n_tiles, tpb, nb, k_shift, out_dtype=F32):
    r, d = xa.shape
    n = w.shape[1]
    return pl.pallas_call(
        functools.partial(_norm_matmul_kernel, tpb=tpb, nb=nb, k_shift=k_shift),
        out_shape=jax.ShapeDtypeStruct((r, n), out_dtype),
        grid=(n_tiles,),
        in_specs=[pl.BlockSpec((ROW_TILE, d), lambda i: (i, 0)),
                  pl.BlockSpec(m.shape, lambda i: (0, 0)),
                  pl.BlockSpec((1, d), lambda i: (0, 0)),
                  pl.BlockSpec(w.shape, lambda i: (0, 0))],
        out_specs=pl.BlockSpec((ROW_TILE, n), lambda i: (i, 0)),
        compiler_params=_cparams(("parallel",)),
        name="norm_matmul",
    )(xa, m, g, w)


def _resid_matmul_kernel(x_ref, a_ref, w_ref, m_ref, o_ref, *, tpb, nb, k_gate):
    d = x_ref.shape[1]
    sel = _row_sel(pl.program_id(0), tpb, nb)
    y = jnp.dot(a_ref[...], w_ref[...], preferred_element_type=F32)
    o_ref[...] = x_ref[...] + _mod_row(m_ref, sel, k_gate, d) * y


def _resid_matmul(xa, a, w, m, *, n_tiles, tpb, nb, k_gate):
    r, d = xa.shape
    return pl.pallas_call(
        functools.partial(_resid_matmul_kernel, tpb=tpb, nb=nb, k_gate=k_gate),
        out_shape=jax.ShapeDtypeStruct((r, d), F32),
        grid=(n_tiles,),
        in_specs=[pl.BlockSpec((ROW_TILE, d), lambda i: (i, 0)),
                  pl.BlockSpec((ROW_TILE, a.shape[1]), lambda i: (i, 0)),
                  pl.BlockSpec(w.shape, lambda i: (0, 0)),
                  pl.BlockSpec(m.shape, lambda i: (0, 0))],
        out_specs=pl.BlockSpec((ROW_TILE, d), lambda i: (i, 0)),
        input_output_aliases={0: 0},
        compiler_params=_cparams(("parallel",)),
        name="resid_matmul",
    )(xa, a, w, m)


def _attn_qkv_kernel(x_ref, m_ref, g_ref, w_ref, qg_ref, kg_ref, cos_ref, sin_ref,
                     q_ref, k_ref, v_ref, *, tpb, nb, qscale):
    d = x_ref.shape[1]
    hd = qg_ref.shape[1]
    sel = _row_sel(pl.program_id(0), tpb, nb)
    h = _modnorm(x_ref[...], g_ref[...], _mod_row(m_ref, sel, 0, d), _mod_row(m_ref, sel, 1, d))
    qkv = jnp.dot(h.astype(BF16), w_ref[...], preferred_element_type=F32)
    cosf = cos_ref[...]
    sinf = sin_ref[...]

    def norm_rope(z, gain):
        zn = z * lax.rsqrt(jnp.mean(z * z, axis=-1, keepdims=True) + EPS) * gain
        return zn * cosf + pltpu.roll(zn, hd // 2, 1) * sinf

    nq = q_ref.shape[1] // hd
    nk = k_ref.shape[1] // hd
    for j in range(nq):
        q_ref[:, j * hd:(j + 1) * hd] = (norm_rope(qkv[:, j * hd:(j + 1) * hd], qg_ref[...]) * qscale).astype(BF16)
    for j in range(nk):
        c0 = (nq + j) * hd
        k_ref[:, j * hd:(j + 1) * hd] = norm_rope(qkv[:, c0:c0 + hd], kg_ref[...]).astype(BF16)
    v_ref[...] = qkv[:, (nq + nk) * hd:].astype(BF16)


def _attn_qkv(xa, m, g, w, qg, kg, cosf, sinf, *, n_tiles, tpb, nb, qscale):
    r, d = xa.shape
    hd = qg.shape[1]
    nq, nk = ATT_HEADS * hd, ATT_KV_HEADS * hd
    tab_map = lambda i: (jnp.where(i < nb * tpb, i % tpb, tpb), 0)
    return pl.pallas_call(
        functools.partial(_attn_qkv_kernel, tpb=tpb, nb=nb, qscale=qscale),
        out_shape=(jax.ShapeDtypeStruct((r, nq), BF16), jax.ShapeDtypeStruct((r, nk), BF16),
                   jax.ShapeDtypeStruct((r, nk), BF16)),
        grid=(n_tiles,),
        in_specs=[pl.BlockSpec((ROW_TILE, d), lambda i: (i, 0)),
                  pl.BlockSpec(m.shape, lambda i: (0, 0)),
                  pl.BlockSpec((1, d), lambda i: (0, 0)),
                  pl.BlockSpec(w.shape, lambda i: (0, 0)),
                  pl.BlockSpec((1, hd), lambda i: (0, 0)),
                  pl.BlockSpec((1, hd), lambda i: (0, 0)),
                  pl.BlockSpec((ROW_TILE, hd), tab_map),
                  pl.BlockSpec((ROW_TILE, hd), tab_map)],
        out_specs=(pl.BlockSpec((ROW_TILE, nq), lambda i: (i, 0)),
                   pl.BlockSpec((ROW_TILE, nk), lambda i: (i, 0)),
                   pl.BlockSpec((ROW_TILE, nk), lambda i: (i, 0))),
        compiler_params=_cparams(("parallel",)),
        name="attn_qkv",
    )(xa, m, g, w, qg, kg, cosf, sinf)


def _flash_kernel(*refs, tq, tk, hd, n_lat_chunks):
    if n_lat_chunks:
        q_ref, kc_ref, vc_ref, kl_ref, vl_ref, o_ref = refs
    else:
        q_ref, kc_ref, vc_ref, o_ref = refs
    qs = jnp.concatenate([q_ref[:, g * hd:(g + 1) * hd] for g in range(ATT_GROUP)], axis=0)
    rows = ATT_GROUP * tq

    def step(kblk, vblk, carry):
        m, l, acc = carry
        s = lax.dot_general(qs, kblk, (((1,), (1,)), ((), ())), preferred_element_type=F32)
        m_new = jnp.maximum(m, jnp.max(s, axis=-1, keepdims=True))
        p = jnp.exp2(s - m_new)
        alpha = jnp.exp2(m - m_new)
        l = alpha * l + jnp.sum(p, axis=-1, keepdims=True)
        acc = alpha * acc + jnp.dot(p.astype(BF16), vblk, preferred_element_type=F32)
        return m_new, l, acc

    carry = (jnp.full((rows, 1), -jnp.inf, F32), jnp.zeros((rows, 1), F32), jnp.zeros((rows, hd), F32))
    carry = step(kc_ref[...], vc_ref[...], carry)
    if n_lat_chunks:
        def body(c, carry):
            off = pl.multiple_of(c * tk, tk)
            return step(kl_ref[pl.ds(off, tk), :], vl_ref[pl.ds(off, tk), :], carry)
        carry = lax.fori_loop(0, n_lat_chunks, body, carry)
    _, l, acc = carry
    o = acc / l
    for g in range(ATT_GROUP):
        o_ref[:, g * hd:(g + 1) * hd] = o[g * tq:(g + 1) * tq].astype(BF16)


def _flash(q, k, v, *, nb, t_lat, n_ctx, hd, lat_queries):
    gw = ATT_GROUP * hd
    ctx_base = nb * t_lat // n_ctx
    kv_ctx = pl.BlockSpec((n_ctx, hd), lambda b, h, i: (ctx_base + b, h))
    if lat_queries:
        tq = min(256, t_lat)
        tk = min(1024, t_lat)
        nq_tiles = t_lat // tq
        q_spec = pl.BlockSpec((tq, gw), lambda b, h, i: (b * nq_tiles + i, h))
        kv_lat = pl.BlockSpec((t_lat, hd), lambda b, h, i: (b, h))
        in_specs = [q_spec, kv_ctx, kv_ctx, kv_lat, kv_lat]
        args = (q, k, v, k, v)
        n_chunks = t_lat // tk
        out_rows = nb * t_lat
    else:
        tq = n_ctx
        tk = n_ctx
        nq_tiles = 1
        q_spec = pl.BlockSpec((tq, gw), lambda b, h, i: (ctx_base + b, h))
        in_specs = [q_spec, kv_ctx, kv_ctx]
        args = (q, k, v)
        n_chunks = 0
        out_rows = nb * n_ctx
    return pl.pallas_call(
        functools.partial(_flash_kernel, tq=tq, tk=tk, hd=hd, n_lat_chunks=n_chunks),
        out_shape=jax.ShapeDtypeStruct((out_rows, q.shape[1]), BF16),
        grid=(nb, ATT_KV_HEADS, nq_tiles),
        in_specs=in_specs,
        out_specs=pl.BlockSpec((tq, gw), lambda b, h, i: (b * nq_tiles + i, h)),
        compiler_params=_cparams(("parallel", "parallel", "parallel")),
        name="flash_lat" if lat_queries else "flash_ctx",
    )(*args)


def _rope_tables(t_lat, hd):
    rows = t_lat // GRID_W
    r_idx, c_idx = jnp.meshgrid(jnp.arange(rows, dtype=F32), jnp.arange(GRID_W, dtype=F32), indexing='ij')
    pairs = hd // 4
    freqs = ROPE_THETA ** (-jnp.arange(pairs, dtype=F32) / pairs)
    ang = jnp.concatenate([r_idx.reshape(-1, 1) * freqs, c_idx.reshape(-1, 1) * freqs], axis=-1)
    cos, sin = jnp.cos(ang), jnp.sin(ang)
    cosf = jnp.concatenate([cos, cos], axis=-1)
    sinf = jnp.concatenate([-sin, sin], axis=-1)
    cosf = jnp.concatenate([cosf, jnp.ones((ROW_TILE, hd), F32)], axis=0)
    sinf = jnp.concatenate([sinf, jnp.zeros((ROW_TILE, hd), F32)], axis=0)
    return cosf, sinf


def _deinterleave_perm(hd):
    return np.concatenate([np.arange(0, hd, 2), np.arange(1, hd, 2)])


def _attention_layer(xa, m, g, w_in, q_gain, k_gain, w_out, geo, ctx_out):
    nb, t_lat, n_ctx, tpb, all_tiles, n_tiles = geo
    hd = q_gain.shape[0]
    perm = _deinterleave_perm(hd)
    nqk = (ATT_HEADS + ATT_KV_HEADS) * hd
    col = np.concatenate([(np.arange(nqk) // hd) * hd + perm[np.arange(nqk) % hd],
                          np.arange(nqk, w_in.shape[1])])
    w = w_in[:, col].astype(BF16)
    cosf, sinf = _rope_tables(t_lat, hd)
    qscale = hd ** -0.5 * math.log2(math.e)
    q, k, v = _attn_qkv(xa, m, g, w, q_gain[perm].reshape(1, hd), k_gain[perm].reshape(1, hd), cosf, sinf,
                        n_tiles=all_tiles, tpb=tpb, nb=nb, qscale=qscale)
    o = _flash(q, k, v, nb=nb, t_lat=t_lat, n_ctx=n_ctx, hd=hd, lat_queries=True)
    if ctx_out:
        o_c = _flash(q, k, v, nb=nb, t_lat=t_lat, n_ctx=n_ctx, hd=hd, lat_queries=False)
        o = jnp.concatenate([o, o_c], axis=0)
    return _resid_matmul(xa, o, w_out.astype(BF16), m, n_tiles=n_tiles, tpb=tpb, nb=nb, k_gate=2)


def _seq_block(b, c, *, nb, nc_ctx, nc_lat, reverse):
    ctx_base = nb * nc_lat
    in_ctx = c < nc_ctx
    cl = c - nc_ctx
    if reverse:
        return jnp.where(in_ctx, ctx_base + b * nc_ctx + (nc_ctx - 1 - c), b * nc_lat + (nc_lat - 1 - cl))
    return jnp.where(in_ctx, ctx_base + b * nc_ctx + c, b * nc_lat + cl)


def _roll_rows(x, s, reverse):
    return pltpu.roll(x, SUBLANES - s if reverse else s, 0)


def _last_row(x, reverse):
    return jnp.broadcast_to(x[0:1, :] if reverse else x[SUBLANES - 1:SUBLANES, :], x.shape)


def _s5_scan_kernel(u_ref, bd_ref, tab_ref, cd_ref, y_ref, hs_ref, st_ref, *, tc, ns, reverse):
    c = pl.program_id(1)

    @pl.when(c == 0)
    def _():
        st_ref[...] = jnp.zeros_like(st_ref)

    hs_ref[...] = jnp.dot(u_ref[...].astype(BF16), bd_ref[0], preferred_element_type=F32)
    ng = tc // SUBLANES

    def body(gi, carry):
        hr, hi = carry
        r0 = pl.multiple_of(((ng - 1 - gi) if reverse else gi) * SUBLANES, SUBLANES)
        br = hs_ref[pl.ds(r0, SUBLANES), 0:ns]
        bi = hs_ref[pl.ds(r0, SUBLANES), ns:2 * ns]
        for k, s in enumerate((1, 2, 4)):
            rr, ri = _roll_rows(br, s, reverse), _roll_rows(bi, s, reverse)
            mr, mi = tab_ref[0, 2 * k], tab_ref[0, 2 * k + 1]
            br, bi = br + mr * rr - mi * ri, bi + mr * ri + mi * rr
        pr, pi_ = tab_ref[0, 6], tab_ref[0, 7]
        nr = br + pr * hr - pi_ * hi
        ni = bi + pr * hi + pi_ * hr
        hs_ref[pl.ds(r0, SUBLANES), 0:ns] = nr
        hs_ref[pl.ds(r0, SUBLANES), ns:2 * ns] = ni
        return _last_row(nr, reverse), _last_row(ni, reverse)

    hr, hi = lax.fori_loop(0, ng, body, (st_ref[:, 0:ns], st_ref[:, ns:2 * ns]))
    st_ref[:, 0:ns] = hr
    st_ref[:, ns:2 * ns] = hi
    y_ref[...] = jnp.dot(hs_ref[...].astype(BF16), cd_ref[0], preferred_element_type=F32)


def _s5_scan(u, bd, tab, cd, geo, direction):
    nb, t_lat, n_ctx = geo[:3]
    r, width = u.shape
    ns = tab.shape[3]
    tc = SEQ_TILE
    nc_ctx, nc_lat = n_ctx // tc, t_lat // tc
    blk = functools.partial(_seq_block, nb=nb, nc_ctx=nc_ctx, nc_lat=nc_lat, reverse=direction == 1)
    return pl.pallas_call(
        functools.partial(_s5_scan_kernel, tc=tc, ns=ns, reverse=direction == 1),
        out_shape=jax.ShapeDtypeStruct((r, width), F32),
        grid=(nb, nc_ctx + nc_lat),
        in_specs=[pl.BlockSpec((tc, width), lambda b, c: (blk(b, c), 0)),
                  pl.BlockSpec((1, width, 2 * ns), lambda b, c: (direction, 0, 0)),
                  pl.BlockSpec((1, 8, SUBLANES, ns), lambda b, c: (direction, 0, 0, 0)),
                  pl.BlockSpec((1, 2 * ns, width), lambda b, c: (direction, 0, 0))],
        out_specs=pl.BlockSpec((tc, width), lambda b, c: (blk(b, c), 0)),
        scratch_shapes=[pltpu.VMEM((tc, 2 * ns), F32), pltpu.VMEM((SUBLANES, 2 * ns), F32)],
        compiler_params=_cparams(("arbitrary", "arbitrary")),
        name="s5_scan_bwd" if direction else "s5_scan_fwd",
    )(u, bd, tab, cd)


def _s5_out_kernel(x_ref, u_ref, yf_ref, yb_ref, dsk_ref, wg_ref, wo_ref, m_ref, o_ref, *, tpb, nb):
    d = x_ref.shape[1]
    sel = _row_sel(pl.program_id(0), tpb, nb)
    y = dsk_ref[...] * u_ref[...] + yf_ref[...] + yb_ref[...]
    z = _gelu(y)
    z = z * jax.nn.sigmoid(jnp.dot(z.astype(BF16), wg_ref[...], preferred_element_type=F32))
    out = jnp.dot(z.astype(BF16), wo_ref[...], preferred_element_type=F32)
    o_ref[...] = x_ref[...] + _mod_row(m_ref, sel, 2, d) * out


def _s5_out(xa, u, yf, yb, dsk, wg, wo, m, *, n_tiles, tpb, nb):
    r, d = xa.shape
    width = u.shape[1]
    return pl.pallas_call(
        functools.partial(_s5_out_kernel, tpb=tpb, nb=nb),
        out_shape=jax.ShapeDtypeStruct((r, d), F32),
        grid=(n_tiles,),
        in_specs=[pl.BlockSpec((ROW_TILE, d), lambda i: (i, 0)),
                  pl.BlockSpec((ROW_TILE, width), lambda i: (i, 0)),
                  pl.BlockSpec((ROW_TILE, width), lambda i: (i, 0)),
                  pl.BlockSpec((ROW_TILE, width), lambda i: (i, 0)),
                  pl.BlockSpec((1, width), lambda i: (0, 0)),
                  pl.BlockSpec(wg.shape, lambda i: (0, 0)),
                  pl.BlockSpec(wo.shape, lambda i: (0, 0)),
                  pl.BlockSpec(m.shape, lambda i: (0, 0))],
        out_specs=pl.BlockSpec((ROW_TILE, d), lambda i: (i, 0)),
        input_output_aliases={0: 0},
        compiler_params=_cparams(("parallel",)),
        name="s5_out",
    )(xa, u, yf, yb, dsk, wg, wo, m)


def _s5_layer(xa, m, g, w_in, lam_re, lam_im, log_dt, b_re, b_im, c_re, c_im, d_skip, w_glu, w_out, geo):
    nb, t_lat, n_ctx, tpb, all_tiles, n_tiles = geo
    groups, p_state = lam_re.shape[1], lam_re.shape[2]
    width = w_in.shape[1]
    ns = groups * p_state
    lam = lax.complex(lam_re.astype(F32), lam_im.astype(F32))
    dt = jnp.exp(log_dt.astype(F32))[..., None]
    lam_bar = jnp.exp(lam * dt)
    b_bar = ((lam_bar - 1.0) / lam)[..., None] * lax.complex(b_re.astype(F32), b_im.astype(F32))
    eye = jnp.eye(groups, dtype=F32)

    def block_diag_in(z):
        return jnp.einsum('dgpc,gh->dgchp', z, eye).reshape(2, groups * SSM_GROUP, ns)

    def block_diag_out(z):
        return jnp.einsum('dgcp,gh->dgphc', z, eye).reshape(2, ns, groups * SSM_GROUP)

    bd = jnp.concatenate([block_diag_in(jnp.real(b_bar)), block_diag_in(jnp.imag(b_bar))], axis=-1).astype(BF16)
    cd = jnp.concatenate([block_diag_out(c_re.astype(F32)), block_diag_out(-c_im.astype(F32))], axis=1).astype(BF16)
    pw = [lam_bar.reshape(2, ns)]
    for _ in range(SUBLANES - 1):
        pw.append(pw[-1] * pw[0])
    row = jnp.arange(SUBLANES)
    tabs = []
    for dirn in range(2):
        rev = dirn == 1
        planes = []
        for s in (1, 2, 4):
            keep = (row + s <= SUBLANES - 1) if rev else (row >= s)
            ms = jnp.where(keep[:, None], pw[s - 1][dirn][None, :], 0.0)
            planes += [jnp.real(ms), jnp.imag(ms)]
        pows = jnp.stack([pw[(SUBLANES - 1 - t) if rev else t][dirn] for t in range(SUBLANES)], axis=0)
        planes += [jnp.real(pows), jnp.imag(pows)]
        tabs.append(jnp.stack(planes, axis=0))
    tab = jnp.stack(tabs, axis=0).astype(F32)

    u = _norm_matmul(xa, m, g, w_in.astype(BF16), n_tiles=all_tiles, tpb=tpb, nb=nb, k_shift=0)
    yf = _s5_scan(u, bd, tab, cd, geo, 0)
    yb = _s5_scan(u, bd, tab, cd, geo, 1)
    return _s5_out(xa, u, yf, yb, d_skip.reshape(1, width).astype(F32), w_glu.astype(BF16), w_out.astype(BF16), m,
                   n_tiles=n_tiles, tpb=tpb, nb=nb)


def _lru_coef_kernel(cur_ref, prev_ref, next_ref, cw_ref, cb_ref, wrg_ref, brg_ref, wig_ref, big_ref, cn_ref,
                     a_ref, b_ref, buf_ref, *, tm, tps_lat, tps_ctx, n_lat_tiles):
    i = pl.program_id(0)
    j = i - n_lat_tiles
    first = jnp.where(i < n_lat_tiles, i % tps_lat == 0, j % tps_ctx == 0)
    last = jnp.where(i < n_lat_tiles, i % tps_lat == tps_lat - 1, j % tps_ctx == tps_ctx - 1)
    h = SUBLANES
    buf_ref[0:h, :] = jnp.where(first, 0.0, prev_ref[...])
    buf_ref[h:h + tm, :] = cur_ref[...]
    buf_ref[h + tm:h + tm + h, :] = jnp.where(last, 0.0, next_ref[...])
    xr = cb_ref[...] + sum(cw_ref[k:k + 1, :] * buf_ref[h - CONV_LEFT + k:h - CONV_LEFT + k + tm, :]
                           for k in range(CONV_W))
    nblk = wrg_ref.shape[1]
    bw = wrg_ref.shape[2]
    for n in range(nblk):
        sl = slice(n * bw, (n + 1) * bw)
        xs = xr[:, sl]
        xb = xs.astype(BF16)
        for d in range(2):
            r = jax.nn.sigmoid(jnp.dot(xb, wrg_ref[d, n], preferred_element_type=F32) + brg_ref[d, :, sl])
            ig = jax.nn.sigmoid(jnp.dot(xb, wig_ref[d, n], preferred_element_type=F32) + big_ref[d, :, sl])
            log_a = cn_ref[d, :, sl] * r
            a_ref[d, :, sl] = jnp.exp(log_a)
            b_ref[d, :, sl] = jnp.sqrt(1.0 - jnp.exp(2.0 * log_a)) * (ig * xs)


def _lru_coef(gx, conv_w, conv_b, wrg, brg, wig, big, cneg, geo):
    nb, t_lat, n_ctx = geo[:3]
    r = gx.shape[0]
    width = conv_w.shape[1]
    tm = SEQ_TILE
    hb = tm // SUBLANES
    n_lat_tiles = nb * t_lat // tm
    last_halo = r // SUBLANES - 1
    vec = lambda a: pl.BlockSpec(a.shape, lambda i: (0,) * a.ndim)
    return pl.pallas_call(
        functools.partial(_lru_coef_kernel, tm=tm, tps_lat=t_lat // tm, tps_ctx=n_ctx // tm, n_lat_tiles=n_lat_tiles),
        out_shape=(jax.ShapeDtypeStruct((2, r, width), F32), jax.ShapeDtypeStruct((2, r, width), F32)),
        grid=(r // tm,),
        in_specs=[pl.BlockSpec((tm, width), lambda i: (i, 1)),
                  pl.BlockSpec((SUBLANES, width), lambda i: (jnp.maximum(i * hb - 1, 0), 1)),
                  pl.BlockSpec((SUBLANES, width), lambda i: (jnp.minimum((i + 1) * hb, last_halo), 1)),
                  vec(conv_w), vec(conv_b), vec(wrg), vec(brg), vec(wig), vec(big), vec(cneg)],
        out_specs=(pl.BlockSpec((2, tm, width), lambda i: (0, i, 0)),
                   pl.BlockSpec((2, tm, width), lambda i: (0, i, 0))),
        scratch_shapes=[pltpu.VMEM((tm + 2 * SUBLANES, width), F32)],
        compiler_params=_cparams(("parallel",)),
        name="lru_coef",
    )(gx, gx, gx, conv_w, conv_b, wrg, brg, wig, big, cneg)


def _lru_scan_kernel(a_ref, b_ref, o_ref, st_ref, *, tc, reverse):
    c = pl.program_id(1)

    @pl.when(c == 0)
    def _():
        st_ref[...] = jnp.zeros_like(st_ref)

    width = a_ref.shape[-1]
    row = lax.broadcasted_iota(jnp.int32, (SUBLANES, width), 0)
    ng = tc // SUBLANES

    def body(gi, h):
        r0 = pl.multiple_of(((ng - 1 - gi) if reverse else gi) * SUBLANES, SUBLANES)
        a = a_ref[0, pl.ds(r0, SUBLANES), :]
        b = b_ref[0, pl.ds(r0, SUBLANES), :]
        for s in (1, 2, 4):
            keep = (row < SUBLANES - s) if reverse else (row >= s)
            ar = jnp.where(keep, _roll_rows(a, s, reverse), 1.0)
            br = jnp.where(keep, _roll_rows(b, s, reverse), 0.0)
            b = a * br + b
            a = a * ar
        hn = a * h + b
        o_ref[pl.ds(r0, SUBLANES), :] = hn
        return _last_row(hn, reverse)

    st_ref[...] = lax.fori_loop(0, ng, body, st_ref[...])


def _lru_scan(a, b, geo, direction):
    nb, t_lat, n_ctx = geo[:3]
    _, r, width = a.shape
    tc = SEQ_TILE
    nc_ctx, nc_lat = n_ctx // tc, t_lat // tc
    blk = functools.partial(_seq_block, nb=nb, nc_ctx=nc_ctx, nc_lat=nc_lat, reverse=direction == 1)
    spec = pl.BlockSpec((1, tc, width), lambda b_, c: (direction, blk(b_, c), 0))
    return pl.pallas_call(
        functools.partial(_lru_scan_kernel, tc=tc, reverse=direction == 1),
        out_shape=jax.ShapeDtypeStruct((r, width), F32),
        grid=(nb, nc_ctx + nc_lat),
        in_specs=[spec, spec],
        out_specs=pl.BlockSpec((tc, width), lambda b_, c: (blk(b_, c), 0)),
        scratch_shapes=[pltpu.VMEM((SUBLANES, width), F32)],
        compiler_params=_cparams(("arbitrary", "arbitrary")),
        name="lru_scan_bwd" if direction else "lru_scan_fwd",
    )(a, b)


def _lru_out_kernel(x_ref, gate_ref, sf_ref, sb_ref, wo_ref, m_ref, o_ref, *, tpb, nb):
    d = x_ref.shape[1]
    sel = _row_sel(pl.program_id(0), tpb, nb)
    y = (sf_ref[...] + sb_ref[...]) * _gelu(gate_ref[...])
    out = jnp.dot(y.astype(BF16), wo_ref[...], preferred_element_type=F32)
    o_ref[...] = x_ref[...] + _mod_row(m_ref, sel, 2, d) * out


def _lru_out(xa, gx, sf, sb, wo, m, *, n_tiles, tpb, nb):
    r, d = xa.shape
    width = sf.shape[1]
    return pl.pallas_call(
        functools.partial(_lru_out_kernel, tpb=tpb, nb=nb),
        out_shape=jax.ShapeDtypeStruct((r, d), F32),
        grid=(n_tiles,),
        in_specs=[pl.BlockSpec((ROW_TILE, d), lambda i: (i, 0)),
                  pl.BlockSpec((ROW_TILE, width), lambda i: (i, 0)),
                  pl.BlockSpec((ROW_TILE, width), lambda i: (i, 0)),
                  pl.BlockSpec((ROW_TILE, width), lambda i: (i, 0)),
                  pl.BlockSpec(wo.shape, lambda i: (0, 0)),
                  pl.BlockSpec(m.shape, lambda i: (0, 0))],
        out_specs=pl.BlockSpec((ROW_TILE, d), lambda i: (i, 0)),
        input_output_aliases={0: 0},
        compiler_params=_cparams(("parallel",)),
        name="lru_out",
    )(xa, gx, sf, sb, wo, m)


def _lru_layer(xa, m, g, w_in, conv_w, conv_b, w_rg, b_rg, w_ig, b_ig, lam, w_out, geo):
    nb, t_lat, n_ctx, tpb, all_tiles, n_tiles = geo
    width = conv_w.shape[1]
    gx = _norm_matmul(xa, m, g, w_in.astype(BF16), n_tiles=all_tiles, tpb=tpb, nb=nb, k_shift=0)
    cneg = (-LRU_C * jax.nn.softplus(-lam.astype(F32))).reshape(2, 1, width)
    a, b = _lru_coef(gx, conv_w.astype(F32), conv_b.reshape(1, width).astype(F32),
                     w_rg.astype(BF16), b_rg.reshape(2, 1, width).astype(F32),
                     w_ig.astype(BF16), b_ig.reshape(2, 1, width).astype(F32), cneg, geo)
    sf = _lru_scan(a, b, geo, 0)
    sb = _lru_scan(a, b, geo, 1)
    return _lru_out(xa, gx, sf, sb, w_out.astype(BF16), m, n_tiles=n_tiles, tpb=tpb, nb=nb)


def _oddeven_merge_sort_pairs(n):
    pairs = []
    p = 1
    while p < n:
        k = p
        while k >= 1:
            for j in range(k % p, n - k, 2 * k):
                for i in range(min(k, n - j - k)):
                    if (i + j) // (2 * p) == (i + j + k) // (2 * p):
                        pairs.append((i + j, i + j + k))
            k //= 2
        p *= 2
    return pairs


_SORT16 = _oddeven_merge_sort_pairs(PEER_TOPK)


def _ce(v, i, j):
    a, b = v[i], v[j]
    v[i] = jnp.maximum(a, b)
    v[j] = jnp.minimum(a, b)


def _bitonic_merge_desc(v):
    n = len(v)
    k = n // 2
    while k >= 1:
        for i in range(n):
            if i & k == 0:
                _ce(v, i, i | k)
        k //= 2


def _top16_desc(blk):
    n = PEER_TOPK
    v = [blk[SUBLANES * a:SUBLANES * (a + 1), :] for a in range(n)]
    for (i, j) in _SORT16:
        _ce(v, i, j)
    for shift in (4, 2, 1):
        o = [pltpu.roll(x, shift, 0) for x in v]
        v = [jnp.maximum(v[i], o[n - 1 - i]) for i in range(n)]
        _bitonic_merge_desc(v)
    return v


_KEY_FLOOR = 2.0 ** -100


def _peer_route(s, nk):
    nh, k = PEER_HEADS, PEER_TOPK
    w = s.shape[1]
    low = nk - 1
    key = lax.broadcasted_iota(jnp.int32, (nk, w), 0)
    sub = lax.broadcasted_iota(jnp.int32, (SUBLANES, w), 0)
    vals, idxs = [], []
    for half in range(2):
        pk = [jnp.zeros((SUBLANES, w), F32) for _ in range(k)]
        for hh in range(nh):
            blk = s[(half * nh + hh) * nk:(half * nh + hh + 1) * nk, :]
            bits = lax.bitcast_convert_type(jnp.where(jnp.abs(blk) < _KEY_FLOOR, _KEY_FLOOR, blk), jnp.int32)
            neg = (bits >> 31) & low
            keys = lax.bitcast_convert_type((bits & ~low) | ((low - key) ^ neg), F32)
            top = _top16_desc(keys)
            pk = [jnp.where(sub == hh, top[i], pk[i]) for i in range(k)]
        pkb = [lax.bitcast_convert_type(x, jnp.int32) for x in pk]
        idxs.append([low - ((x & low) ^ ((x >> 31) & low)) for x in pkb])
        vals.append([lax.bitcast_convert_type(x & ~low, F32) for x in pkb])
    pairs = [(i, j) for i in range(k) for j in range(k) if (i + 1) * (j + 1) <= k]
    cur = [vals[0][i] + vals[1][j] for (i, j) in pairs]
    best, codes = [], []
    for _ in range(k):
        mx = functools.reduce(jnp.maximum, cur)
        hc = functools.reduce(jnp.minimum, [jnp.where(c == mx, i * k + j, k * k) for c, (i, j) in zip(cur, pairs)])
        cur = [jnp.where(hc == i * k + j, -jnp.inf, c) for c, (i, j) in zip(cur, pairs)]
        best.append(mx)
        codes.append(hc)
    ex = [jnp.exp(b - best[0]) for b in best]
    zinv = 1.0 / functools.reduce(lambda a, b: a + b, ex)
    i1, i2 = [], []
    for hc in codes:
        ri, rj = hc >> (k.bit_length() - 1), hc & (k - 1)
        i1.append(functools.reduce(lambda a, b: a + b, [jnp.where(ri == i, idxs[0][i], 0) for i in range(k)]))
        i2.append(functools.reduce(lambda a, b: a + b, [jnp.where(rj == j, idxs[1][j], 0) for j in range(k)]))
    cat = lambda xs: jnp.concatenate(xs, axis=0)
    return cat(i1), cat(i2), cat([e * zinv for e in ex])


def _peer_score_kernel(x_ref, m_ref, g_ref, wq_ref, kk_ref, h_ref, i1_ref, i2_ref, gate_ref, s_ref, *, tpb, nb):
    d = x_ref.shape[1]
    nk = kk_ref.shape[1]
    kd = kk_ref.shape[2]
    sel = _row_sel(pl.program_id(0), tpb, nb)
    h = _modnorm(x_ref[...], g_ref[...], _mod_row(m_ref, sel, 3, d), _mod_row(m_ref, sel, 4, d)).astype(BF16)
    h_ref[...] = h
    q = jnp.dot(h, wq_ref[...], preferred_element_type=F32).astype(BF16)
    for half in range(2):
        for hh in range(PEER_HEADS):
            jq = hh * 2 + half
            blk = half * PEER_HEADS + hh
            s_ref[blk * nk:(blk + 1) * nk, :] = lax.dot_general(
                kk_ref[blk], q[:, jq * kd:(jq + 1) * kd], (((1,), (1,)), ((), ())),
                preferred_element_type=F32)
    i1, i2, gate = _peer_route(s_ref, nk)
    i1_ref[...] = i1.astype(F32).T
    i2_ref[...] = i2.astype(F32).T
    gate_ref[...] = gate.T


def _peer_scores(xa, m, g, wq, kk, *, n_tiles, tpb, nb):
    r, d = xa.shape
    nk = kk.shape[1]
    slots = PEER_HEADS * PEER_TOPK
    slot_spec = pl.BlockSpec((ROW_TILE, slots), lambda i: (i, 0))
    slot_shape = jax.ShapeDtypeStruct((r, slots), F32)
    return pl.pallas_call(
        functools.partial(_peer_score_kernel, tpb=tpb, nb=nb),
        out_shape=(jax.ShapeDtypeStruct((r, d), BF16), slot_shape, slot_shape, slot_shape),
        grid=(n_tiles,),
        in_specs=[pl.BlockSpec((ROW_TILE, d), lambda i: (i, 0)),
                  pl.BlockSpec(m.shape, lambda i: (0, 0)),
                  pl.BlockSpec((1, d), lambda i: (0, 0)),
                  pl.BlockSpec(wq.shape, lambda i: (0, 0)),
                  pl.BlockSpec(kk.shape, lambda i: (0, 0, 0))],
        out_specs=(pl.BlockSpec((ROW_TILE, d), lambda i: (i, 0)), slot_spec, slot_spec, slot_spec),
        scratch_shapes=[pltpu.VMEM((2 * PEER_HEADS * nk, ROW_TILE), F32)],
        compiler_params=_cparams(("parallel",)),
        name="peer_scores",
    )(xa, m, g, wq, kk)


PEER_PASSES = 2
PEER_EXPERT_BLOCK = 1024


def _peer_dense_kernel(x_ref, h_ref, i1_ref, i2_ref, gate_ref, ut_ref, v_ref, m_ref, o_ref,
                       acc_ref, w_ref, *, tpb, nb, nk, pitch):
    d = x_ref.shape[1]
    tm = x_ref.shape[0]
    eb = ut_ref.shape[1]
    n_i1 = eb // nk
    rows = nk // PEER_PASSES
    steps = rows // n_i1
    j = pl.program_id(1)
    jj = j % steps

    @pl.when(j == 0)
    def _():
        acc_ref[...] = jnp.zeros_like(acc_ref)

    @pl.when(jj == 0)
    def _():
        first = (j // steps) * rows
        slots = i1_ref.shape[1]
        key1 = (lax.broadcasted_iota(jnp.int32, (rows, slots), 0) + first).astype(F32)
        key2 = lax.broadcasted_iota(jnp.int32, (nk, slots), 0).astype(F32)

        def chunk(ci, carry):
            r0 = pl.multiple_of(ci * SUBLANES, SUBLANES)
            i1c = i1_ref[pl.ds(r0, SUBLANES), :]
            i2c = i2_ref[pl.ds(r0, SUBLANES), :]
            gc = gate_ref[pl.ds(r0, SUBLANES), :]
            ghi = gc.astype(BF16).astype(F32)
            glo = gc - ghi
            for r in range(SUBLANES):
                hit1 = key1 == i1c[r:r + 1, :]
                lhs = jnp.concatenate([jnp.where(hit1, ghi[r:r + 1, :], 0.0),
                                       jnp.where(hit1, glo[r:r + 1, :], 0.0)], axis=1).astype(BF16)
                hit2 = jnp.where(key2 == i2c[r:r + 1, :], 1.0, 0.0)
                rhs = jnp.concatenate([hit2, hit2], axis=1).astype(BF16)
                wt = lax.dot_general(lhs, rhs, (((1,), (1,)), ((), ())), preferred_element_type=F32)
                w_ref[pl.ds(pl.multiple_of((r0 + r) * pitch, SUBLANES), rows), :] = wt
            return carry

        lax.fori_loop(0, tm // SUBLANES, chunk, 0, unroll=4)

    s = jnp.dot(h_ref[...], ut_ref[...], preferred_element_type=F32)
    p = []
    for ii in range(n_i1):
        wb = w_ref[pl.ds(jj * n_i1 + ii, tm, stride=pitch), :]
        p.append((wb * _gelu(s[:, ii * nk:(ii + 1) * nk])).astype(BF16))
    acc_ref[...] += jnp.dot(jnp.concatenate(p, axis=1), v_ref[...], preferred_element_type=F32)

    @pl.when(j == pl.num_programs(1) - 1)
    def _():
        sel = _row_sel(pl.program_id(0), tpb, nb)
        o_ref[...] = x_ref[...] + _mod_row(m_ref, sel, 5, d) * acc_ref[...]


def _peer_dense(xa, h, i1, i2, gate, ut, v, m, *, n_tiles, tpb, nb, nk):
    r, d = xa.shape
    n_exp = v.shape[0]
    eb = PEER_EXPERT_BLOCK
    tm = ROW_TILE
    slots = i1.shape[1]
    rows = nk // PEER_PASSES
    assert nk % PEER_PASSES == 0 and rows % (eb // nk) == 0 and n_exp == nk * nk
    pitch = rows + SUBLANES if (rows // SUBLANES) % 2 == 0 else rows
    slot_spec = pl.BlockSpec((tm, slots), lambda i, j: (i, 0))
    return pl.pallas_call(
        functools.partial(_peer_dense_kernel, tpb=tpb, nb=nb, nk=nk, pitch=pitch),
        out_shape=jax.ShapeDtypeStruct((r, d), F32),
        grid=(n_tiles, n_exp // eb),
        in_specs=[pl.BlockSpec((tm, d), lambda i, j: (i, 0)),
                  pl.BlockSpec((tm, d), lambda i, j: (i, 0)),
                  slot_spec, slot_spec, slot_spec,
                  pl.BlockSpec((d, eb), lambda i, j: (0, j)),
                  pl.BlockSpec((eb, d), lambda i, j: (j, 0)),
                  pl.BlockSpec(m.shape, lambda i, j: (0, 0))],
        out_specs=pl.BlockSpec((tm, d), lambda i, j: (i, 0)),
        scratch_shapes=[pltpu.VMEM((tm, d), F32),
                        pltpu.VMEM((tm * pitch, nk), F32)],
        input_output_aliases={0: 0},
        compiler_params=pltpu.CompilerParams(dimension_semantics=("parallel", "arbitrary"),
                                             vmem_limit_bytes=PEER_VMEM_LIMIT),
        name="peer_dense",
    )(xa, h, i1, i2, gate, ut, v, m)


def _peer_layer(xa, m, g, w_q, k1, k2, u_tab, v_tab, geo):
    nb, t_lat, n_ctx, tpb, all_tiles, n_tiles = geo
    nk = k1.shape[1]
    kk = jnp.concatenate([k1, k2], axis=0).astype(BF16)
    h, i1, i2, gate = _peer_scores(xa, m, g, w_q.astype(BF16), kk, n_tiles=n_tiles, tpb=tpb, nb=nb)
    return _peer_dense(xa, h, i1, i2, gate, u_tab.astype(BF16).T, v_tab.astype(BF16), m,
                       n_tiles=n_tiles, tpb=tpb, nb=nb, nk=nk)


def _final_norm_kernel(x_ref, g_ref, o_ref):
    x = x_ref[...]
    o_ref[...] = x * lax.rsqrt(jnp.mean(x * x, axis=-1, keepdims=True) + EPS) * g_ref[...]


def _final_norm(xa, g, n_rows):
    d = xa.shape[1]
    return pl.pallas_call(
        _final_norm_kernel,
        out_shape=jax.ShapeDtypeStruct((n_rows, d), F32),
        grid=(n_rows // ROW_TILE,),
        in_specs=[pl.BlockSpec((ROW_TILE, d), lambda i: (i, 0)), pl.BlockSpec((1, d), lambda i: (0, 0))],
        out_specs=pl.BlockSpec((ROW_TILE, d), lambda i: (i, 0)),
        compiler_params=_cparams(("parallel",)),
        name="final_norm",
    )(xa, g)


def kernel(x, c, ctx, c_ctx, mod_w, mod_b, norm_mix, norm_ffn, norm_final, attn_w_in, attn_q_gain, attn_k_gain, attn_w_out, ssm_w_in, ssm_lam_re, ssm_lam_im, ssm_log_dt, ssm_b_re, ssm_b_im, ssm_c_re, ssm_c_im, ssm_d, ssm_w_glu, ssm_w_out, lru_w_in, lru_conv_w, lru_conv_b, lru_w_rg, lru_b_rg, lru_w_ig, lru_b_ig, lru_lam, lru_w_out, peer_w_q, peer_k1, peer_k2, peer_u, peer_v):
    nb, t_lat, d = x.shape
    n_ctx = ctx.shape[1]
    depth = mod_w.shape[0]
    assert t_lat % ROW_TILE == 0 and (nb * n_ctx) % ROW_TILE == 0 and n_ctx % SEQ_TILE == 0
    assert nb + 1 <= SUBLANES and t_lat % GRID_W == 0
    tpb = t_lat // ROW_TILE
    lat_tiles = nb * tpb
    all_tiles = lat_tiles + nb * n_ctx // ROW_TILE

    xa = jnp.concatenate([x.reshape(nb * t_lat, d), ctx.reshape(nb * n_ctx, d)], axis=0).astype(F32)
    crows = jnp.concatenate([c.astype(F32), c_ctx.reshape(1, d).astype(F32),
                             jnp.zeros((SUBLANES - nb - 1, d), F32)], axis=0)
    mods = _modulation(crows, mod_w.astype(F32), mod_b.astype(F32))

    for i in range(depth):
        ctx_out = i < depth - 1
        geo = (nb, t_lat, n_ctx, tpb, all_tiles, all_tiles if ctx_out else lat_tiles)
        kind, j = i % N_MIXERS, i // N_MIXERS
        m = mods[i]
        g_mix = norm_mix[i].reshape(1, d).astype(F32)
        if kind == 0:
            xa = _attention_layer(xa, m, g_mix, attn_w_in[j], attn_q_gain[j], attn_k_gain[j], attn_w_out[j],
                                  geo, ctx_out)
        elif kind == 1:
            xa = _s5_layer(xa, m, g_mix, ssm_w_in[j], ssm_lam_re[j], ssm_lam_im[j], ssm_log_dt[j],
                           ssm_b_re[j], ssm_b_im[j], ssm_c_re[j], ssm_c_im[j], ssm_d[j],
                           ssm_w_glu[j], ssm_w_out[j], geo)
        else:
            xa = _lru_layer(xa, m, g_mix, lru_w_in[j], lru_conv_w[j], lru_conv_b[j], lru_w_rg[j], lru_b_rg[j],
                            lru_w_ig[j], lru_b_ig[j], lru_lam[j], lru_w_out[j], geo)
        xa = _peer_layer(xa, m, norm_ffn[i].reshape(1, d).astype(F32), peer_w_q[i], peer_k1[i], peer_k2[i],
                         peer_u[i], peer_v[i], geo)
    out = _final_norm(xa, norm_final.reshape(1, d).astype(F32), nb * t_lat)
    return out.reshape(nb, t_lat, d)
```

```python
import functools
import math

import numpy as np
import jax
import jax.numpy as jnp
from jax import lax
from jax.experimental import pallas as pl
from jax.experimental.pallas import tpu as pltpu

F32 = jnp.float32
BF16 = jnp.bfloat16

EPS = 1e-6
GRID_W = 64
ROPE_THETA = 10000.0
ATT_HEADS = 8
ATT_KV_HEADS = 2
ATT_GROUP = ATT_HEADS // ATT_KV_HEADS
SSM_GROUP = 16
LRU_BLOCKS = 8
CONV_W = 4
CONV_LEFT = 2
LRU_C = 8.0
PEER_HEADS = 8
PEER_TOPK = 16
N_MIXERS = 3

LANES = 128
SUBLANES = 8
VMEM_LIMIT = 48 * 1024 * 1024
PEER_VMEM_LIMIT = 56 * 1024 * 1024

ROW_TILE = 512
SEQ_TILE = 256


def _cparams(sem):
    return pltpu.CompilerParams(dimension_semantics=sem, vmem_limit_bytes=VMEM_LIMIT)


def _gelu(x):
    return 0.5 * x * (1.0 + lax.erf(x * (1.0 / math.sqrt(2.0))))


def _modnorm(x, g, shift, scale):
    y = x * lax.rsqrt(jnp.mean(x * x, axis=-1, keepdims=True) + EPS) * g
    return y * (1.0 + scale) + shift


def _mod_row(m_ref, sel, k, d):
    return m_ref[pl.ds(sel, 1), k * d:(k + 1) * d]


def _row_sel(i, tiles_per_batch, n_batch):
    return jnp.minimum(i // tiles_per_batch, n_batch)


def _mod_kernel(c_ref, w_ref, b_ref, o_ref):
    c = c_ref[...]
    s = c * jax.nn.sigmoid(c)
    o_ref[0] = jnp.dot(s.astype(BF16), w_ref[0].astype(BF16), preferred_element_type=F32) + b_ref[0]


def _modulation(crows, mod_w, mod_b):
    depth, d, n = mod_w.shape
    tn = 1536
    return pl.pallas_call(
        _mod_kernel,
        out_shape=jax.ShapeDtypeStruct((depth, SUBLANES, n), F32),
        grid=(depth, n // tn),
        in_specs=[pl.BlockSpec((SUBLANES, d), lambda l, j: (0, 0)),
                  pl.BlockSpec((1, d, tn), lambda l, j: (l, 0, j)),
                  pl.BlockSpec((1, 1, tn), lambda l, j: (l, 0, j))],
        out_specs=pl.BlockSpec((1, SUBLANES, tn), lambda l, j: (l, 0, j)),
        compiler_params=_cparams(("parallel", "parallel")),
        name="modulation",
    )(crows, mod_w, mod_b.reshape(depth, 1, n))


def _norm_matmul_kernel(x_ref, m_ref, g_ref, w_ref, o_ref, *, tpb, nb, k_shift):
    d = x_ref.shape[1]
    sel = _row_sel(pl.program_id(0), tpb, nb)
    h = _modnorm(x_ref[...], g_ref[...], _mod_row(m_ref, sel, k_shift, d), _mod_row(m_ref, sel, k_shift + 1, d))
    o_ref[...] = jnp.dot(h.astype(BF16), w_ref[...], preferred_element_type=F32).astype(o_ref.dtype)


def _norm_matmul(xa, m, g, w, *, n_tiles, tpb, nb, k_shift, out_dtype=F32):
    r, d = xa.shape
    n = w.shape[1]
    return pl.pallas_call(
        functools.partial(_norm_matmul_kernel, tpb=tpb, nb=nb, k_shift=k_shift),
        out_shape=jax.ShapeDtypeStruct((r, n), out_dtype),
        grid=(n_tiles,),
        in_specs=[pl.BlockSpec((ROW_TILE, d), lambda i: (i, 0)),
                  pl.BlockSpec(m.shape, lambda i: (0, 0)),
                  pl.BlockSpec((1, d), lambda i: (0, 0)),
                  pl.BlockSpec(w.shape, lambda i: (0, 0))],
        out_specs=pl.BlockSpec((ROW_TILE, n), lambda i: (i, 0)),
        compiler_params=_cparams(("parallel",)),
        name="norm_matmul",
    )(xa, m, g, w)


def _resid_matmul_kernel(x_ref, a_ref, w_ref, m_ref, o_ref, *, tpb, nb, k_gate):
    d = x_ref.shape[1]
    sel = _row_sel(pl.program_id(0), tpb, nb)
    y = jnp.dot(a_ref[...], w_ref[...], preferred_element_type=F32)
    o_ref[...] = x_ref[...] + _mod_row(m_ref, sel, k_gate, d) * y


def _resid_matmul(xa, a, w, m, *, n_tiles, tpb, nb, k_gate):
    r, d = xa.shape
    return pl.pallas_call(
        functools.partial(_resid_matmul_kernel, tpb=tpb, nb=nb, k_gate=k_gate),
        out_shape=jax.ShapeDtypeStruct((r, d), F32),
        grid=(n_tiles,),
        in_specs=[pl.BlockSpec((ROW_TILE, d), lambda i: (i, 0)),
                  pl.BlockSpec((ROW_TILE, a.shape[1]), lambda i: (i, 0)),
                  pl.BlockSpec(w.shape, lambda i: (0, 0)),
                  pl.BlockSpec(m.shape, lambda i: (0, 0))],
        out_specs=pl.BlockSpec((ROW_TILE, d), lambda i: (i, 0)),
        input_output_aliases={0: 0},
        compiler_params=_cparams(("parallel",)),
        name="resid_matmul",
    )(xa, a, w, m)


def _attn_qkv_kernel(x_ref, m_ref, g_ref, w_ref, qg_ref, kg_ref, cos_ref, sin_ref,
                     q_ref, k_ref, v_ref, *, tpb, nb, qscale):
    d = x_ref.shape[1]
    hd = qg_ref.shape[1]
    sel = _row_sel(pl.program_id(0), tpb, nb)
    h = _modnorm(x_ref[...], g_ref[...], _mod_row(m_ref, sel, 0, d), _mod_row(m_ref, sel, 1, d))
    qkv = jnp.dot(h.astype(BF16), w_ref[...], preferred_element_type=F32)
    cosf = cos_ref[...]
    sinf = sin_ref[...]

    def norm_rope(z, gain):
        zn = z * lax.rsqrt(jnp.mean(z * z, axis=-1, keepdims=True) + EPS) * gain
        return zn * cosf + pltpu.roll(zn, hd // 2, 1) * sinf

    nq = q_ref.shape[1] // hd
    nk = k_ref.shape[1] // hd
    for j in range(nq):
        q_ref[:, j * hd:(j + 1) * hd] = (norm_rope(qkv[:, j * hd:(j + 1) * hd], qg_ref[...]) * qscale).astype(BF16)
    for j in range(nk):
        c0 = (nq + j) * hd
        k_ref[:, j * hd:(j + 1) * hd] = norm_rope(qkv[:, c0:c0 + hd], kg_ref[...]).astype(BF16)
    v_ref[...] = qkv[:, (nq + nk) * hd:].astype(BF16)


def _attn_qkv(xa, m, g, w, qg, kg, cosf, sinf, *, n_tiles, tpb, nb, qscale):
    r, d = xa.shape
    hd = qg.shape[1]
    nq, nk = ATT_HEADS * hd, ATT_KV_HEADS * hd
    tab_map = lambda i: (jnp.where(i < nb * tpb, i % tpb, tpb), 0)
    return pl.pallas_call(
        functools.partial(_attn_qkv_kernel, tpb=tpb, nb=nb, qscale=qscale),
        out_shape=(jax.ShapeDtypeStruct((r, nq), BF16), jax.ShapeDtypeStruct((r, nk), BF16),
                   jax.ShapeDtypeStruct((r, nk), BF16)),
        grid=(n_tiles,),
        in_specs=[pl.BlockSpec((ROW_TILE, d), lambda i: (i, 0)),
                  pl.BlockSpec(m.shape, lambda i: (0, 0)),
                  pl.BlockSpec((1, d), lambda i: (0, 0)),
                  pl.BlockSpec(w.shape, lambda i: (0, 0)),
                  pl.BlockSpec((1, hd), lambda i: (0, 0)),
                  pl.BlockSpec((1, hd), lambda i: (0, 0)),
                  pl.BlockSpec((ROW_TILE, hd), tab_map),
                  pl.BlockSpec((ROW_TILE, hd), tab_map)],
        out_specs=(pl.BlockSpec((ROW_TILE, nq), lambda i: (i, 0)),
                   pl.BlockSpec((ROW_TILE, nk), lambda i: (i, 0)),
                   pl.BlockSpec((ROW_TILE, nk), lambda i: (i, 0))),
        compiler_params=_cparams(("parallel",)),
        name="attn_qkv",
    )(xa, m, g, w, qg, kg, cosf, sinf)


def _flash_kernel(*refs, tq, tk, hd, n_lat_chunks):
    if n_lat_chunks:
        q_ref, kc_ref, vc_ref, kl_ref, vl_ref, o_ref, sa_ref, sb_ref = refs
    else:
        q_ref, kc_ref, vc_ref, o_ref = refs
    qs = jnp.concatenate([q_ref[:, g * hd:(g + 1) * hd] for g in range(ATT_GROUP)], axis=0)
    rows = ATT_GROUP * tq

    def scores(kblk):
        return lax.dot_general(qs, kblk, (((1,), (1,)), ((), ())), preferred_element_type=F32)

    def absorb(s, vblk, carry):
        m, l, acc = carry
        m_new = jnp.maximum(m, jnp.max(s, axis=-1, keepdims=True))
        p = jnp.exp2(s - m_new)
        alpha = jnp.exp2(m - m_new)
        l = alpha * l + jnp.sum(p, axis=-1, keepdims=True)
        acc = alpha * acc + jnp.dot(p.astype(BF16), vblk, preferred_element_type=F32)
        return m_new, l, acc

    def lat(ref, c):
        return ref[pl.ds(pl.multiple_of(c * tk, tk), tk), :]

    carry = (jnp.full((rows, 1), -jnp.inf, F32), jnp.zeros((rows, 1), F32), jnp.zeros((rows, hd), F32))
    carry = absorb(scores(kc_ref[...]), vc_ref[...], carry)
    if n_lat_chunks == 1:
        carry = absorb(scores(lat(kl_ref, 0)), lat(vl_ref, 0), carry)
    elif n_lat_chunks:
        assert n_lat_chunks % 2 == 0
        sa_ref[...] = scores(lat(kl_ref, 0))

        def body(c2, carry):
            c = 2 * c2
            sb_ref[...] = scores(lat(kl_ref, c + 1))
            carry = absorb(sa_ref[...], lat(vl_ref, c), carry)
            sa_ref[...] = scores(lat(kl_ref, jnp.minimum(c + 2, n_lat_chunks - 1)))
            return absorb(sb_ref[...], lat(vl_ref, c + 1), carry)

        carry = lax.fori_loop(0, n_lat_chunks // 2, body, carry)
    _, l, acc = carry
    o = acc / l
    for g in range(ATT_GROUP):
        o_ref[:, g * hd:(g + 1) * hd] = o[g * tq:(g + 1) * tq].astype(BF16)


def _flash(q, k, v, *, nb, t_lat, n_ctx, hd, lat_queries):
    gw = ATT_GROUP * hd
    ctx_base = nb * t_lat // n_ctx
    kv_ctx = pl.BlockSpec((n_ctx, hd), lambda b, h, i: (ctx_base + b, h))
    if lat_queries:
        tq = min(256, t_lat)
        tk = min(1024, t_lat)
        nq_tiles = t_lat // tq
        q_spec = pl.BlockSpec((tq, gw), lambda b, h, i: (b * nq_tiles + i, h))
        kv_lat = pl.BlockSpec((t_lat, hd), lambda b, h, i: (b, h))
        in_specs = [q_spec, kv_ctx, kv_ctx, kv_lat, kv_lat]
        args = (q, k, v, k, v)
        n_chunks = t_lat // tk
        out_rows = nb * t_lat
    else:
        tq = n_ctx
        tk = n_ctx
        nq_tiles = 1
        q_spec = pl.BlockSpec((tq, gw), lambda b, h, i: (ctx_base + b, h))
        in_specs = [q_spec, kv_ctx, kv_ctx]
        args = (q, k, v)
        n_chunks = 0
        out_rows = nb * n_ctx
    return pl.pallas_call(
        functools.partial(_flash_kernel, tq=tq, tk=tk, hd=hd, n_lat_chunks=n_chunks),
        out_shape=jax.ShapeDtypeStruct((out_rows, q.shape[1]), BF16),
        grid=(nb, ATT_KV_HEADS, nq_tiles),
        in_specs=in_specs,
        out_specs=pl.BlockSpec((tq, gw), lambda b, h, i: (b * nq_tiles + i, h)),
        scratch_shapes=[pltpu.VMEM((ATT_GROUP * tq, tk), F32)] * 2 if n_chunks else [],
        compiler_params=_cparams(("parallel", "parallel", "parallel")),
        name="flash_lat" if lat_queries else "flash_ctx",
    )(*args)


def _rope_tables(t_lat, hd):
    rows = t_lat // GRID_W
    r_idx, c_idx = jnp.meshgrid(jnp.arange(rows, dtype=F32), jnp.arange(GRID_W, dtype=F32), indexing='ij')
    pairs = hd // 4
    freqs = ROPE_THETA ** (-jnp.arange(pairs, dtype=F32) / pairs)
    ang = jnp.concatenate([r_idx.reshape(-1, 1) * freqs, c_idx.reshape(-1, 1) * freqs], axis=-1)
    cos, sin = jnp.cos(ang), jnp.sin(ang)
    cosf = jnp.concatenate([cos, cos], axis=-1)
    sinf = jnp.concatenate([-sin, sin], axis=-1)
    cosf = jnp.concatenate([cosf, jnp.ones((ROW_TILE, hd), F32)], axis=0)
    sinf = jnp.concatenate([sinf, jnp.zeros((ROW_TILE, hd), F32)], axis=0)
    return cosf, sinf


def _deinterleave_perm(hd):
    return np.concatenate([np.arange(0, hd, 2), np.arange(1, hd, 2)])


def _attention_layer(xa, m, g, w_in, q_gain, k_gain, w_out, geo, ctx_out):
    nb, t_lat, n_ctx, tpb, all_tiles, n_tiles = geo
    hd = q_gain.shape[0]
    perm = _deinterleave_perm(hd)
    nqk = (ATT_HEADS + ATT_KV_HEADS) * hd
    col = np.concatenate([(np.arange(nqk) // hd) * hd + perm[np.arange(nqk) % hd],
                          np.arange(nqk, w_in.shape[1])])
    w = w_in[:, col].astype(BF16)
    cosf, sinf = _rope_tables(t_lat, hd)
    qscale = hd ** -0.5 * math.log2(math.e)
    q, k, v = _attn_qkv(xa, m, g, w, q_gain[perm].reshape(1, hd), k_gain[perm].reshape(1, hd), cosf, sinf,
                        n_tiles=all_tiles, tpb=tpb, nb=nb, qscale=qscale)
    o = _flash(q, k, v, nb=nb, t_lat=t_lat, n_ctx=n_ctx, hd=hd, lat_queries=True)
    if ctx_out:
        o_c = _flash(q, k, v, nb=nb, t_lat=t_lat, n_ctx=n_ctx, hd=hd, lat_queries=False)
        o = jnp.concatenate([o, o_c], axis=0)
    return _resid_matmul(xa, o, w_out.astype(BF16), m, n_tiles=n_tiles, tpb=tpb, nb=nb, k_gate=2)


def _seq_block(b, c, *, nb, nc_ctx, nc_lat, reverse):
    ctx_base = nb * nc_lat
    in_ctx = c < nc_ctx
    cl = c - nc_ctx
    if reverse:
        return jnp.where(in_ctx, ctx_base + b * nc_ctx + (nc_ctx - 1 - c), b * nc_lat + (nc_lat - 1 - cl))
    return jnp.where(in_ctx, ctx_base + b * nc_ctx + c, b * nc_lat + cl)


def _roll_rows(x, s, reverse):
    return pltpu.roll(x, SUBLANES - s if reverse else s, 0)


def _last_row(x, reverse):
    return jnp.broadcast_to(x[0:1, :] if reverse else x[SUBLANES - 1:SUBLANES, :], x.shape)


def _s5_scan_kernel(u_ref, bd_ref, tab_ref, cd_ref, y_ref, hs_ref, st_ref, *, tc, ns, reverse):
    c = pl.program_id(1)

    @pl.when(c == 0)
    def _():
        st_ref[...] = jnp.zeros_like(st_ref)

    hs_ref[...] = jnp.dot(u_ref[...].astype(BF16), bd_ref[0], preferred_element_type=F32)
    ng = tc // SUBLANES

    def body(gi, carry):
        hr, hi = carry
        r0 = pl.multiple_of(((ng - 1 - gi) if reverse else gi) * SUBLANES, SUBLANES)
        br = hs_ref[pl.ds(r0, SUBLANES), 0:ns]
        bi = hs_ref[pl.ds(r0, SUBLANES), ns:2 * ns]
        for k, s in enumerate((1, 2, 4)):
            rr, ri = _roll_rows(br, s, reverse), _roll_rows(bi, s, reverse)
            mr, mi = tab_ref[0, 2 * k], tab_ref[0, 2 * k + 1]
            br, bi = br + mr * rr - mi * ri, bi + mr * ri + mi * rr
        pr, pi_ = tab_ref[0, 6], tab_ref[0, 7]
        nr = br + pr * hr - pi_ * hi
        ni = bi + pr * hi + pi_ * hr
        hs_ref[pl.ds(r0, SUBLANES), 0:ns] = nr
        hs_ref[pl.ds(r0, SUBLANES), ns:2 * ns] = ni
        return _last_row(nr, reverse), _last_row(ni, reverse)

    hr, hi = lax.fori_loop(0, ng, body, (st_ref[:, 0:ns], st_ref[:, ns:2 * ns]))
    st_ref[:, 0:ns] = hr
    st_ref[:, ns:2 * ns] = hi
    y_ref[...] = jnp.dot(hs_ref[...].astype(BF16), cd_ref[0], preferred_element_type=F32)


def _s5_scan(u, bd, tab, cd, geo, direction):
    nb, t_lat, n_ctx = geo[:3]
    r, width = u.shape
    ns = tab.shape[3]
    tc = SEQ_TILE
    nc_ctx, nc_lat = n_ctx // tc, t_lat // tc
    blk = functools.partial(_seq_block, nb=nb, nc_ctx=nc_ctx, nc_lat=nc_lat, reverse=direction == 1)
    return pl.pallas_call(
        functools.partial(_s5_scan_kernel, tc=tc, ns=ns, reverse=direction == 1),
        out_shape=jax.ShapeDtypeStruct((r, width), F32),
        grid=(nb, nc_ctx + nc_lat),
        in_specs=[pl.BlockSpec((tc, width), lambda b, c: (blk(b, c), 0)),
                  pl.BlockSpec((1, width, 2 * ns), lambda b, c: (direction, 0, 0)),
                  pl.BlockSpec((1, 8, SUBLANES, ns), lambda b, c: (direction, 0, 0, 0)),
                  pl.BlockSpec((1, 2 * ns, width), lambda b, c: (direction, 0, 0))],
        out_specs=pl.BlockSpec((tc, width), lambda b, c: (blk(b, c), 0)),
        scratch_shapes=[pltpu.VMEM((tc, 2 * ns), F32), pltpu.VMEM((SUBLANES, 2 * ns), F32)],
        compiler_params=_cparams(("arbitrary", "arbitrary")),
        name="s5_scan_bwd" if direction else "s5_scan_fwd",
    )(u, bd, tab, cd)


def _s5_out_kernel(x_ref, u_ref, yf_ref, yb_ref, dsk_ref, wg_ref, wo_ref, m_ref, o_ref, *, tpb, nb):
    d = x_ref.shape[1]
    sel = _row_sel(pl.program_id(0), tpb, nb)
    y = dsk_ref[...] * u_ref[...] + yf_ref[...] + yb_ref[...]
    z = _gelu(y)
    z = z * jax.nn.sigmoid(jnp.dot(z.astype(BF16), wg_ref[...], preferred_element_type=F32))
    out = jnp.dot(z.astype(BF16), wo_ref[...], preferred_element_type=F32)
    o_ref[...] = x_ref[...] + _mod_row(m_ref, sel, 2, d) * out


def _s5_out(xa, u, yf, yb, dsk, wg, wo, m, *, n_tiles, tpb, nb):
    r, d = xa.shape
    width = u.shape[1]
    return pl.pallas_call(
        functools.partial(_s5_out_kernel, tpb=tpb, nb=nb),
        out_shape=jax.ShapeDtypeStruct((r, d), F32),
        grid=(n_tiles,),
        in_specs=[pl.BlockSpec((ROW_TILE, d), lambda i: (i, 0)),
                  pl.BlockSpec((ROW_TILE, width), lambda i: (i, 0)),
                  pl.BlockSpec((ROW_TILE, width), lambda i: (i, 0)),
                  pl.BlockSpec((ROW_TILE, width), lambda i: (i, 0)),
                  pl.BlockSpec((1, width), lambda i: (0, 0)),
                  pl.BlockSpec(wg.shape, lambda i: (0, 0)),
                  pl.BlockSpec(wo.shape, lambda i: (0, 0)),
                  pl.BlockSpec(m.shape, lambda i: (0, 0))],
        out_specs=pl.BlockSpec((ROW_TILE, d), lambda i: (i, 0)),
        input_output_aliases={0: 0},
        compiler_params=_cparams(("parallel",)),
        name="s5_out",
    )(xa, u, yf, yb, dsk, wg, wo, m)


def _s5_layer(xa, m, g, w_in, lam_re, lam_im, log_dt, b_re, b_im, c_re, c_im, d_skip, w_glu, w_out, geo):
    nb, t_lat, n_ctx, tpb, all_tiles, n_tiles = geo
    groups, p_state = lam_re.shape[1], lam_re.shape[2]
    width = w_in.shape[1]
    ns = groups * p_state
    lam = lax.complex(lam_re.astype(F32), lam_im.astype(F32))
    dt = jnp.exp(log_dt.astype(F32))[..., None]
    lam_bar = jnp.exp(lam * dt)
    b_bar = ((lam_bar - 1.0) / lam)[..., None] * lax.complex(b_re.astype(F32), b_im.astype(F32))
    eye = jnp.eye(groups, dtype=F32)

    def block_diag_in(z):
        return jnp.einsum('dgpc,gh->dgchp', z, eye).reshape(2, groups * SSM_GROUP, ns)

    def block_diag_out(z):
        return jnp.einsum('dgcp,gh->dgphc', z, eye).reshape(2, ns, groups * SSM_GROUP)

    bd = jnp.concatenate([block_diag_in(jnp.real(b_bar)), block_diag_in(jnp.imag(b_bar))], axis=-1).astype(BF16)
    cd = jnp.concatenate([block_diag_out(c_re.astype(F32)), block_diag_out(-c_im.astype(F32))], axis=1).astype(BF16)
    pw = [lam_bar.reshape(2, ns)]
    for _ in range(SUBLANES - 1):
        pw.append(pw[-1] * pw[0])
    row = jnp.arange(SUBLANES)
    tabs = []
    for dirn in range(2):
        rev = dirn == 1
        planes = []
        for s in (1, 2, 4):
            keep = (row + s <= SUBLANES - 1) if rev else (row >= s)
            ms = jnp.where(keep[:, None], pw[s - 1][dirn][None, :], 0.0)
            planes += [jnp.real(ms), jnp.imag(ms)]
        pows = jnp.stack([pw[(SUBLANES - 1 - t) if rev else t][dirn] for t in range(SUBLANES)], axis=0)
        planes += [jnp.real(pows), jnp.imag(pows)]
        tabs.append(jnp.stack(planes, axis=0))
    tab = jnp.stack(tabs, axis=0).astype(F32)

    u = _norm_matmul(xa, m, g, w_in.astype(BF16), n_tiles=all_tiles, tpb=tpb, nb=nb, k_shift=0)
    yf = _s5_scan(u, bd, tab, cd, geo, 0)
    yb = _s5_scan(u, bd, tab, cd, geo, 1)
    return _s5_out(xa, u, yf, yb, d_skip.reshape(1, width).astype(F32), w_glu.astype(BF16), w_out.astype(BF16), m,
                   n_tiles=n_tiles, tpb=tpb, nb=nb)


def _lru_coef_kernel(cur_ref, prev_ref, next_ref, cw_ref, cb_ref, wrg_ref, brg_ref, wig_ref, big_ref, cn_ref,
                     a_ref, b_ref, buf_ref, *, tm, tps_lat, tps_ctx, n_lat_tiles):
    i = pl.program_id(0)
    j = i - n_lat_tiles
    first = jnp.where(i < n_lat_tiles, i % tps_lat == 0, j % tps_ctx == 0)
    last = jnp.where(i < n_lat_tiles, i % tps_lat == tps_lat - 1, j % tps_ctx == tps_ctx - 1)
    h = SUBLANES
    buf_ref[0:h, :] = jnp.where(first, 0.0, prev_ref[...])
    buf_ref[h:h + tm, :] = cur_ref[...]
    buf_ref[h + tm:h + tm + h, :] = jnp.where(last, 0.0, next_ref[...])
    xr = cb_ref[...] + sum(cw_ref[k:k + 1, :] * buf_ref[h - CONV_LEFT + k:h - CONV_LEFT + k + tm, :]
                           for k in range(CONV_W))
    nblk = wrg_ref.shape[1]
    bw = wrg_ref.shape[2]
    for n in range(nblk):
        sl = slice(n * bw, (n + 1) * bw)
        xs = xr[:, sl]
        xb = xs.astype(BF16)
        for d in range(2):
            r = jax.nn.sigmoid(jnp.dot(xb, wrg_ref[d, n], preferred_element_type=F32) + brg_ref[d, :, sl])
            ig = jax.nn.sigmoid(jnp.dot(xb, wig_ref[d, n], preferred_element_type=F32) + big_ref[d, :, sl])
            log_a = cn_ref[d, :, sl] * r
            a_ref[d, :, sl] = jnp.exp(log_a)
            b_ref[d, :, sl] = jnp.sqrt(1.0 - jnp.exp(2.0 * log_a)) * (ig * xs)


def _lru_coef(gx, conv_w, conv_b, wrg, brg, wig, big, cneg, geo):
    nb, t_lat, n_ctx = geo[:3]
    r = gx.shape[0]
    width = conv_w.shape[1]
    tm = SEQ_TILE
    hb = tm // SUBLANES
    n_lat_tiles = nb * t_lat // tm
    last_halo = r // SUBLANES - 1
    vec = lambda a: pl.BlockSpec(a.shape, lambda i: (0,) * a.ndim)
    return pl.pallas_call(
        functools.partial(_lru_coef_kernel, tm=tm, tps_lat=t_lat // tm, tps_ctx=n_ctx // tm, n_lat_tiles=n_lat_tiles),
        out_shape=(jax.ShapeDtypeStruct((2, r, width), F32), jax.ShapeDtypeStruct((2, r, width), F32)),
        grid=(r // tm,),
        in_specs=[pl.BlockSpec((tm, width), lambda i: (i, 1)),
                  pl.BlockSpec((SUBLANES, width), lambda i: (jnp.maximum(i * hb - 1, 0), 1)),
                  pl.BlockSpec((SUBLANES, width), lambda i: (jnp.minimum((i + 1) * hb, last_halo), 1)),
                  vec(conv_w), vec(conv_b), vec(wrg), vec(brg), vec(wig), vec(big), vec(cneg)],
        out_specs=(pl.BlockSpec((2, tm, width), lambda i: (0, i, 0)),
                   pl.BlockSpec((2, tm, width), lambda i: (0, i, 0))),
        scratch_shapes=[pltpu.VMEM((tm + 2 * SUBLANES, width), F32)],
        compiler_params=_cparams(("parallel",)),
        name="lru_coef",
    )(gx, gx, gx, conv_w, conv_b, wrg, brg, wig, big, cneg)


def _lru_scan_kernel(a_ref, b_ref, o_ref, st_ref, *, tc, reverse):
    c = pl.program_id(1)

    @pl.when(c == 0)
    def _():
        st_ref[...] = jnp.zeros_like(st_ref)

    width = a_ref.shape[-1]
    row = lax.broadcasted_iota(jnp.int32, (SUBLANES, width), 0)
    ng = tc // SUBLANES

    def body(gi, h):
        r0 = pl.multiple_of(((ng - 1 - gi) if reverse else gi) * SUBLANES, SUBLANES)
        a = a_ref[0, pl.ds(r0, SUBLANES), :]
        b = b_ref[0, pl.ds(r0, SUBLANES), :]
        for s in (1, 2, 4):
            keep = (row < SUBLANES - s) if reverse else (row >= s)
            ar = jnp.where(keep, _roll_rows(a, s, reverse), 1.0)
            br = jnp.where(keep, _roll_rows(b, s, reverse), 0.0)
            b = a * br + b
            a = a * ar
        hn = a * h + b
        o_ref[pl.ds(r0, SUBLANES), :] = hn
        return _last_row(hn, reverse)

    st_ref[...] = lax.fori_loop(0, ng, body, st_ref[...])


def _lru_scan(a, b, geo, direction):
    nb, t_lat, n_ctx = geo[:3]
    _, r, width = a.shape
    tc = SEQ_TILE
    nc_ctx, nc_lat = n_ctx // tc, t_lat // tc
    blk = functools.partial(_seq_block, nb=nb, nc_ctx=nc_ctx, nc_lat=nc_lat, reverse=direction == 1)
    spec = pl.BlockSpec((1, tc, width), lambda b_, c: (direction, blk(b_, c), 0))
    return pl.pallas_call(
        functools.partial(_lru_scan_kernel, tc=tc, reverse=direction == 1),
        out_shape=jax.ShapeDtypeStruct((r, width), F32),
        grid=(nb, nc_ctx + nc_lat),
        in_specs=[spec, spec],
        out_specs=pl.BlockSpec((tc, width), lambda b_, c: (blk(b_, c), 0)),
        scratch_shapes=[pltpu.VMEM((SUBLANES, width), F32)],
        compiler_params=_cparams(("arbitrary", "arbitrary")),
        name="lru_scan_bwd" if direction else "lru_scan_fwd",
    )(a, b)


def _lru_out_kernel(x_ref, gate_ref, sf_ref, sb_ref, wo_ref, m_ref, o_ref, *, tpb, nb):
    d = x_ref.shape[1]
    sel = _row_sel(pl.program_id(0), tpb, nb)
    y = (sf_ref[...] + sb_ref[...]) * _gelu(gate_ref[...])
    out = jnp.dot(y.astype(BF16), wo_ref[...], preferred_element_type=F32)
    o_ref[...] = x_ref[...] + _mod_row(m_ref, sel, 2, d) * out


def _lru_out(xa, gx, sf, sb, wo, m, *, n_tiles, tpb, nb):
    r, d = xa.shape
    width = sf.shape[1]
    return pl.pallas_call(
        functools.partial(_lru_out_kernel, tpb=tpb, nb=nb),
        out_shape=jax.ShapeDtypeStruct((r, d), F32),
        grid=(n_tiles,),
        in_specs=[pl.BlockSpec((ROW_TILE, d), lambda i: (i, 0)),
                  pl.BlockSpec((ROW_TILE, width), lambda i: (i, 0)),
                  pl.BlockSpec((ROW_TILE, width), lambda i: (i, 0)),
                  pl.BlockSpec((ROW_TILE, width), lambda i: (i, 0)),
                  pl.BlockSpec(wo.shape, lambda i: (0, 0)),
                  pl.BlockSpec(m.shape, lambda i: (0, 0))],
        out_specs=pl.BlockSpec((ROW_TILE, d), lambda i: (i, 0)),
        input_output_aliases={0: 0},
        compiler_params=_cparams(("parallel",)),
        name="lru_out",
    )(xa, gx, sf, sb, wo, m)


def _lru_layer(xa, m, g, w_in, conv_w, conv_b, w_rg, b_rg, w_ig, b_ig, lam, w_out, geo):
    nb, t_lat, n_ctx, tpb, all_tiles, n_tiles = geo
    width = conv_w.shape[1]
    gx = _norm_matmul(xa, m, g, w_in.astype(BF16), n_tiles=all_tiles, tpb=tpb, nb=nb, k_shift=0)
    cneg = (-LRU_C * jax.nn.softplus(-lam.astype(F32))).reshape(2, 1, width)
    a, b = _lru_coef(gx, conv_w.astype(F32), conv_b.reshape(1, width).astype(F32),
                     w_rg.astype(BF16), b_rg.reshape(2, 1, width).astype(F32),
                     w_ig.astype(BF16), b_ig.reshape(2, 1, width).astype(F32), cneg, geo)
    sf = _lru_scan(a, b, geo, 0)
    sb = _lru_scan(a, b, geo, 1)
    return _lru_out(xa, gx, sf, sb, w_out.astype(BF16), m, n_tiles=n_tiles, tpb=tpb, nb=nb)


def _oddeven_merge_sort_pairs(n):
    pairs = []
    p = 1
    while p < n:
        k = p
        while k >= 1:
            for j in range(k % p, n - k, 2 * k):
                for i in range(min(k, n - j - k)):
                    if (i + j) // (2 * p) == (i + j + k) // (2 * p):
                        pairs.append((i + j, i + j + k))
            k //= 2
        p *= 2
    return pairs


_SORT16 = _oddeven_merge_sort_pairs(PEER_TOPK)


def _ce(v, i, j):
    a, b = v[i], v[j]
    v[i] = jnp.maximum(a, b)
    v[j] = jnp.minimum(a, b)


def _bitonic_merge_desc(v):
    n = len(v)
    k = n // 2
    while k >= 1:
        for i in range(n):
            if i & k == 0:
                _ce(v, i, i | k)
        k //= 2


def _top16_desc(blk):
    n = PEER_TOPK
    v = [blk[SUBLANES * a:SUBLANES * (a + 1), :] for a in range(n)]
    for (i, j) in _SORT16:
        _ce(v, i, j)
    for shift in (4, 2, 1):
        o = [pltpu.roll(x, shift, 0) for x in v]
        v = [jnp.maximum(v[i], o[n - 1 - i]) for i in range(n)]
        _bitonic_merge_desc(v)
    return v


_KEY_FLOOR = 2.0 ** -100


def _peer_route(s, nk):
    nh, k = PEER_HEADS, PEER_TOPK
    w = s.shape[1]
    low = nk - 1
    key = lax.broadcasted_iota(jnp.int32, (nk, w), 0)
    sub = lax.broadcasted_iota(jnp.int32, (SUBLANES, w), 0)
    vals, idxs = [], []
    for half in range(2):
        pk = [jnp.zeros((SUBLANES, w), F32) for _ in range(k)]
        for hh in range(nh):
            blk = s[(half * nh + hh) * nk:(half * nh + hh + 1) * nk, :]
            bits = lax.bitcast_convert_type(jnp.where(jnp.abs(blk) < _KEY_FLOOR, _KEY_FLOOR, blk), jnp.int32)
            neg = (bits >> 31) & low
            keys = lax.bitcast_convert_type((bits & ~low) | ((low - key) ^ neg), F32)
            top = _top16_desc(keys)
            pk = [jnp.where(sub == hh, top[i], pk[i]) for i in range(k)]
        pkb = [lax.bitcast_convert_type(x, jnp.int32) for x in pk]
        idxs.append([low - ((x & low) ^ ((x >> 31) & low)) for x in pkb])
        vals.append([lax.bitcast_convert_type(x & ~low, F32) for x in pkb])
    pairs = [(i, j) for i in range(k) for j in range(k) if (i + 1) * (j + 1) <= k]
    cur = [vals[0][i] + vals[1][j] for (i, j) in pairs]
    best, codes = [], []
    for _ in range(k):
        mx = functools.reduce(jnp.maximum, cur)
        hc = functools.reduce(jnp.minimum, [jnp.where(c == mx, i * k + j, k * k) for c, (i, j) in zip(cur, pairs)])
        cur = [jnp.where(hc == i * k + j, -jnp.inf, c) for c, (i, j) in zip(cur, pairs)]
        best.append(mx)
        codes.append(hc)
    ex = [jnp.exp(b - best[0]) for b in best]
    zinv = 1.0 / functools.reduce(lambda a, b: a + b, ex)
    i1, i2 = [], []
    for hc in codes:
        ri, rj = hc >> (k.bit_length() - 1), hc & (k - 1)
        i1.append(functools.reduce(lambda a, b: a + b, [jnp.where(ri == i, idxs[0][i], 0) for i in range(k)]))
        i2.append(functools.reduce(lambda a, b: a + b, [jnp.where(rj == j, idxs[1][j], 0) for j in range(k)]))
    cat = lambda xs: jnp.concatenate(xs, axis=0)
    return cat(i1), cat(i2), cat([e * zinv for e in ex])


def _peer_score_kernel(x_ref, m_ref, g_ref, wq_ref, kk_ref, h_ref, i1_ref, i2_ref, gate_ref, s_ref, *, tpb, nb):
    d = x_ref.shape[1]
    nk = kk_ref.shape[1]
    kd = kk_ref.shape[2]
    sel = _row_sel(pl.program_id(0), tpb, nb)
    h = _modnorm(x_ref[...], g_ref[...], _mod_row(m_ref, sel, 3, d), _mod_row(m_ref, sel, 4, d)).astype(BF16)
    h_ref[...] = h
    q = jnp.dot(h, wq_ref[...], preferred_element_type=F32).astype(BF16)
    for half in range(2):
        for hh in range(PEER_HEADS):
            jq = hh * 2 + half
            blk = half * PEER_HEADS + hh
            s_ref[blk * nk:(blk + 1) * nk, :] = lax.dot_general(
                kk_ref[blk], q[:, jq * kd:(jq + 1) * kd], (((1,), (1,)), ((), ())),
                preferred_element_type=F32)
    i1, i2, gate = _peer_route(s_ref, nk)
    i1_ref[...] = i1.astype(F32).T
    i2_ref[...] = i2.astype(F32).T
    gate_ref[...] = gate.T


def _peer_scores(xa, m, g, wq, kk, *, n_tiles, tpb, nb):
    r, d = xa.shape
    nk = kk.shape[1]
    slots = PEER_HEADS * PEER_TOPK
    slot_spec = pl.BlockSpec((ROW_TILE, slots), lambda i: (i, 0))
    slot_shape = jax.ShapeDtypeStruct((r, slots), F32)
    return pl.pallas_call(
        functools.partial(_peer_score_kernel, tpb=tpb, nb=nb),
        out_shape=(jax.ShapeDtypeStruct((r, d), BF16), slot_shape, slot_shape, slot_shape),
        grid=(n_tiles,),
        in_specs=[pl.BlockSpec((ROW_TILE, d), lambda i: (i, 0)),
                  pl.BlockSpec(m.shape, lambda i: (0, 0)),
                  pl.BlockSpec((1, d), lambda i: (0, 0)),
                  pl.BlockSpec(wq.shape, lambda i: (0, 0)),
                  pl.BlockSpec(kk.shape, lambda i: (0, 0, 0))],
        out_specs=(pl.BlockSpec((ROW_TILE, d), lambda i: (i, 0)), slot_spec, slot_spec, slot_spec),
        scratch_shapes=[pltpu.VMEM((2 * PEER_HEADS * nk, ROW_TILE), F32)],
        compiler_params=_cparams(("parallel",)),
        name="peer_scores",
    )(xa, m, g, wq, kk)


PEER_PASSES = 2
PEER_EXPERT_BLOCK = 2048


def _peer_dense_kernel(x_ref, h_ref, i1_ref, i2_ref, gate_ref, ut_ref, v_ref, m_ref, o_ref,
                       acc_ref, w_ref, *, tpb, nb, nk, pitch):
    d = x_ref.shape[1]
    tm = x_ref.shape[0]
    eb = ut_ref.shape[1]
    n_i1 = eb // nk
    rows = nk // PEER_PASSES
    steps = rows // n_i1
    j = pl.program_id(1)
    jj = j % steps

    @pl.when(j == 0)
    def _():
        acc_ref[...] = jnp.zeros_like(acc_ref)

    @pl.when(jj == 0)
    def _():
        first = (j // steps) * rows
        slots = i1_ref.shape[1]
        key1 = (lax.broadcasted_iota(jnp.int32, (rows, slots), 0) + first).astype(F32)
        key2 = lax.broadcasted_iota(jnp.int32, (nk, slots), 0).astype(F32)

        def chunk(ci, carry):
            r0 = pl.multiple_of(ci * SUBLANES, SUBLANES)
            i1c = i1_ref[pl.ds(r0, SUBLANES), :]
            i2c = i2_ref[pl.ds(r0, SUBLANES), :]
            gc = gate_ref[pl.ds(r0, SUBLANES), :]
            for r in range(SUBLANES):
                lhs = jnp.where(key1 == i1c[r:r + 1, :], gc[r:r + 1, :], 0.0).astype(BF16)
                rhs = jnp.where(key2 == i2c[r:r + 1, :], 1.0, 0.0).astype(BF16)
                wt = lax.dot_general(lhs, rhs, (((1,), (1,)), ((), ())), preferred_element_type=F32)
                w_ref[pl.ds(pl.multiple_of((r0 + r) * pitch, SUBLANES), rows), :] = wt
            return carry

        lax.fori_loop(0, tm // SUBLANES, chunk, 0, unroll=8)

    s = jnp.dot(h_ref[...], ut_ref[...], preferred_element_type=F32)
    p = []
    for ii in range(n_i1):
        wb = w_ref[pl.ds(jj * n_i1 + ii, tm, stride=pitch), :]
        p.append((wb * _gelu(s[:, ii * nk:(ii + 1) * nk])).astype(BF16))
    acc_ref[...] += jnp.dot(jnp.concatenate(p, axis=1), v_ref[...], preferred_element_type=F32)

    @pl.when(j == pl.num_programs(1) - 1)
    def _():
        sel = _row_sel(pl.program_id(0), tpb, nb)
        o_ref[...] = x_ref[...] + _mod_row(m_ref, sel, 5, d) * acc_ref[...]


def _peer_dense(xa, h, i1, i2, gate, ut, v, m, *, n_tiles, tpb, nb, nk):
    r, d = xa.shape
    n_exp = v.shape[0]
    eb = PEER_EXPERT_BLOCK
    tm = ROW_TILE
    slots = i1.shape[1]
    rows = nk // PEER_PASSES
    assert nk % PEER_PASSES == 0 and rows % (eb // nk) == 0 and n_exp == nk * nk
    pitch = rows + SUBLANES if (rows // SUBLANES) % 2 == 0 else rows
    slot_spec = pl.BlockSpec((tm, slots), lambda i, j: (i, 0))
    return pl.pallas_call(
        functools.partial(_peer_dense_kernel, tpb=tpb, nb=nb, nk=nk, pitch=pitch),
        out_shape=jax.ShapeDtypeStruct((r, d), F32),
        grid=(n_tiles, n_exp // eb),
        in_specs=[pl.BlockSpec((tm, d), lambda i, j: (i, 0)),
                  pl.BlockSpec((tm, d), lambda i, j: (i, 0)),
                  slot_spec, slot_spec, slot_spec,
                  pl.BlockSpec((d, eb), lambda i, j: (0, j)),
                  pl.BlockSpec((eb, d), lambda i, j: (j, 0)),
                  pl.BlockSpec(m.shape, lambda i, j: (0, 0))],
        out_specs=pl.BlockSpec((tm, d), lambda i, j: (i, 0)),
        scratch_shapes=[pltpu.VMEM((tm, d), F32),
                        pltpu.VMEM((tm * pitch, nk), F32)],
        input_output_aliases={0: 0},
        compiler_params=pltpu.CompilerParams(dimension_semantics=("parallel", "arbitrary"),
                                             vmem_limit_bytes=PEER_VMEM_LIMIT),
        name="peer_dense",
    )(xa, h, i1, i2, gate, ut, v, m)


def _peer_layer(xa, m, g, w_q, k1, k2, u_tab, v_tab, geo):
    nb, t_lat, n_ctx, tpb, all_tiles, n_tiles = geo
    nk = k1.shape[1]
    kk = jnp.concatenate([k1, k2], axis=0).astype(BF16)
    h, i1, i2, gate = _peer_scores(xa, m, g, w_q.astype(BF16), kk, n_tiles=n_tiles, tpb=tpb, nb=nb)
    return _peer_dense(xa, h, i1, i2, gate, u_tab.astype(BF16).T, v_tab.astype(BF16), m,
                       n_tiles=n_tiles, tpb=tpb, nb=nb, nk=nk)


def _final_norm_kernel(x_ref, g_ref, o_ref):
    x = x_ref[...]
    o_ref[...] = x * lax.rsqrt(jnp.mean(x * x, axis=-1, keepdims=True) + EPS) * g_ref[...]


def _final_norm(xa, g, n_rows):
    d = xa.shape[1]
    return pl.pallas_call(
        _final_norm_kernel,
        out_shape=jax.ShapeDtypeStruct((n_rows, d), F32),
        grid=(n_rows // ROW_TILE,),
        in_specs=[pl.BlockSpec((ROW_TILE, d), lambda i: (i, 0)), pl.BlockSpec((1, d), lambda i: (0, 0))],
        out_specs=pl.BlockSpec((ROW_TILE, d), lambda i: (i, 0)),
        compiler_params=_cparams(("parallel",)),
        name="final_norm",
    )(xa, g)


def kernel(x, c, ctx, c_ctx, mod_w, mod_b, norm_mix, norm_ffn, norm_final, attn_w_in, attn_q_gain, attn_k_gain, attn_w_out, ssm_w_in, ssm_lam_re, ssm_lam_im, ssm_log_dt, ssm_b_re, ssm_b_im, ssm_c_re, ssm_c_im, ssm_d, ssm_w_glu, ssm_w_out, lru_w_in, lru_conv_w, lru_conv_b, lru_w_rg, lru_b_rg, lru_w_ig, lru_b_ig, lru_lam, lru_w_out, peer_w_q, peer_k1, peer_k2, peer_u, peer_v):
    nb, t_lat, d = x.shape
    n_ctx = ctx.shape[1]
    depth = mod_w.shape[0]
    assert t_lat % ROW_TILE == 0 and (nb * n_ctx) % ROW_TILE == 0 and n_ctx % SEQ_TILE == 0
    assert nb + 1 <= SUBLANES and t_lat % GRID_W == 0
    tpb = t_lat // ROW_TILE
    lat_tiles = nb * tpb
    all_tiles = lat_tiles + nb * n_ctx // ROW_TILE

    xa = jnp.concatenate([x.reshape(nb * t_lat, d), ctx.reshape(nb * n_ctx, d)], axis=0).astype(F32)
    crows = jnp.concatenate([c.astype(F32), c_ctx.reshape(1, d).astype(F32),
                             jnp.zeros((SUBLANES - nb - 1, d), F32)], axis=0)
    mods = _modulation(crows, mod_w.astype(F32), mod_b.astype(F32))

    for i in range(depth):
        ctx_out = i < depth - 1
        geo = (nb, t_lat, n_ctx, tpb, all_tiles, all_tiles if ctx_out else lat_tiles)
        kind, j = i % N_MIXERS, i // N_MIXERS
        m = mods[i]
        g_mix = norm_mix[i].reshape(1, d).astype(F32)
        if kind == 0:
            xa = _attention_layer(xa, m, g_mix, attn_w_in[j], attn_q_gain[j], attn_k_gain[j], attn_w_out[j],
                                  geo, ctx_out)
        elif kind == 1:
            xa = _s5_layer(xa, m, g_mix, ssm_w_in[j], ssm_lam_re[j], ssm_lam_im[j], ssm_log_dt[j],
                           ssm_b_re[j], ssm_b_im[j], ssm_c_re[j], ssm_c_im[j], ssm_d[j],
                           ssm_w_glu[j], ssm_w_out[j], geo)
        else:
            xa = _lru_layer(xa, m, g_mix, lru_w_in[j], lru_conv_w[j], lru_conv_b[j], lru_w_rg[j], lru_b_rg[j],
                            lru_w_ig[j], lru_b_ig[j], lru_lam[j], lru_w_out[j], geo)
        xa = _peer_layer(xa, m, norm_ffn[i].reshape(1, d).astype(F32), peer_w_q[i], peer_k1[i], peer_k2[i],
                         peer_u[i], peer_v[i], geo)
    out = _final_norm(xa, norm_final.reshape(1, d).astype(F32), nb * t_lat)
    return out.reshape(nb, t_lat, d)
```

```python
import functools
import math

import numpy as np
import jax
import jax.numpy as jnp
from jax import lax
from jax.experimental import pallas as pl
from jax.experimental.pallas import tpu as pltpu

F32 = jnp.float32
BF16 = jnp.bfloat16

EPS = 1e-6
GRID_W = 64
ROPE_THETA = 10000.0
ATT_HEADS = 8
ATT_KV_HEADS = 2
ATT_GROUP = ATT_HEADS // ATT_KV_HEADS
SSM_GROUP = 16
LRU_BLOCKS = 8
CONV_W = 4
CONV_LEFT = 2
LRU_C = 8.0
PEER_HEADS = 8
PEER_TOPK = 16
N_MIXERS = 3

LANES = 128
SUBLANES = 8
VMEM_LIMIT = 48 * 1024 * 1024
PEER_VMEM_LIMIT = 56 * 1024 * 1024

ROW_TILE = 512
SEQ_TILE = 256


def _cparams(sem):
    return pltpu.CompilerParams(dimension_semantics=sem, vmem_limit_bytes=VMEM_LIMIT)


def _gelu(x):
    return 0.5 * x * (1.0 + lax.erf(x * (1.0 / math.sqrt(2.0))))


def _modnorm(x, g, shift, scale):
    y = x * lax.rsqrt(jnp.mean(x * x, axis=-1, keepdims=True) + EPS) * g
    return y * (1.0 + scale) + shift


def _mod_row(m_ref, sel, k, d):
    return m_ref[pl.ds(sel, 1), k * d:(k + 1) * d]


def _row_sel(i, tiles_per_batch, n_batch):
    return jnp.minimum(i // tiles_per_batch, n_batch)


def _mod_kernel(c_ref, w_ref, b_ref, o_ref):
    c = c_ref[...]
    s = c * jax.nn.sigmoid(c)
    o_ref[0] = jnp.dot(s.astype(BF16), w_ref[0].astype(BF16), preferred_element_type=F32) + b_ref[0]


def _modulation(crows, mod_w, mod_b):
    depth, d, n = mod_w.shape
    tn = 1536
    return pl.pallas_call(
        _mod_kernel,
        out_shape=jax.ShapeDtypeStruct((depth, SUBLANES, n), F32),
        grid=(depth, n // tn),
        in_specs=[pl.BlockSpec((SUBLANES, d), lambda l, j: (0, 0)),
                  pl.BlockSpec((1, d, tn), lambda l, j: (l, 0, j)),
                  pl.BlockSpec((1, 1, tn), lambda l, j: (l, 0, j))],
        out_specs=pl.BlockSpec((1, SUBLANES, tn), lambda l, j: (l, 0, j)),
        compiler_params=_cparams(("parallel", "parallel")),
        name="modulation",
    )(crows, mod_w, mod_b.reshape(depth, 1, n))


def _norm_matmul_kernel(x_ref, m_ref, g_ref, w_ref, o_ref, *, tpb, nb, k_shift):
    d = x_ref.shape[1]
    sel = _row_sel(pl.program_id(0), tpb, nb)
    h = _modnorm(x_ref[...], g_ref[...], _mod_row(m_ref, sel, k_shift, d), _mod_row(m_ref, sel, k_shift + 1, d))
    o_ref[...] = jnp.dot(h.astype(BF16), w_ref[...], preferred_element_type=F32).astype(o_ref.dtype)


def _norm_matmul(xa, m, g, w, *, n_tiles, tpb, nb, k_shift, out_dtype=F32):
    r, d = xa.shape
    n = w.shape[1]
    return pl.pallas_call(
        functools.partial(_norm_matmul_kernel, tpb=tpb, nb=nb, k_shift=k_shift),
        out_shape=jax.ShapeDtypeStruct((r, n), out_dtype),
        grid=(n_tiles,),
        in_specs=[pl.BlockSpec((ROW_TILE, d), lambda i: (i, 0)),
                  pl.BlockSpec(m.shape, lambda i: (0, 0)),
                  pl.BlockSpec((1, d), lambda i: (0, 0)),
                  pl.BlockSpec(w.shape, lambda i: (0, 0))],
        out_specs=pl.BlockSpec((ROW_TILE, n), lambda i: (i, 0)),
        compiler_params=_cparams(("parallel",)),
        name="norm_matmul",
    )(xa, m, g, w)


def _resid_matmul_kernel(x_ref, a_ref, w_ref, m_ref, o_ref, *, tpb, nb, k_gate):
    d = x_ref.shape[1]
    sel = _row_sel(pl.program_id(0), tpb, nb)
    y = jnp.dot(a_ref[...], w_ref[...], preferred_element_type=F32)
    o_ref[...] = x_ref[...] + _mod_row(m_ref, sel, k_gate, d) * y


def _resid_matmul(xa, a, w, m, *, n_tiles, tpb, nb, k_gate):
    r, d = xa.shape
    return pl.pallas_call(
        functools.partial(_resid_matmul_kernel, tpb=tpb, nb=nb, k_gate=k_gate),
        out_shape=jax.ShapeDtypeStruct((r, d), F32),
        grid=(n_tiles,),
        in_specs=[pl.BlockSpec((ROW_TILE, d), lambda i: (i, 0)),
                  pl.BlockSpec((ROW_TILE, a.shape[1]), lambda i: (i, 0)),
                  pl.BlockSpec(w.shape, lambda i: (0, 0)),
                  pl.BlockSpec(m.shape, lambda i: (0, 0))],
        out_specs=pl.BlockSpec((ROW_TILE, d), lambda i: (i, 0)),
        input_output_aliases={0: 0},
        compiler_params=_cparams(("parallel",)),
        name="resid_matmul",
    )(xa, a, w, m)


def _attn_qkv_kernel(x_ref, m_ref, g_ref, w_ref, qg_ref, kg_ref, cos_ref, sin_ref,
                     q_ref, k_ref, v_ref, *, tpb, nb, qscale):
    d = x_ref.shape[1]
    hd = qg_ref.shape[1]
    sel = _row_sel(pl.program_id(0), tpb, nb)
    h = _modnorm(x_ref[...], g_ref[...], _mod_row(m_ref, sel, 0, d), _mod_row(m_ref, sel, 1, d))
    qkv = jnp.dot(h.astype(BF16), w_ref[...], preferred_element_type=F32)
    cosf = cos_ref[...]
    sinf = sin_ref[...]

    def norm_rope(z, gain):
        zn = z * lax.rsqrt(jnp.mean(z * z, axis=-1, keepdims=True) + EPS) * gain
        return zn * cosf + pltpu.roll(zn, hd // 2, 1) * sinf

    nq = q_ref.shape[1] // hd
    nk = k_ref.shape[1] // hd
    for j in range(nq):
        q_ref[:, j * hd:(j + 1) * hd] = (norm_rope(qkv[:, j * hd:(j + 1) * hd], qg_ref[...]) * qscale).astype(BF16)
    for j in range(nk):
        c0 = (nq + j) * hd
        k_ref[:, j * hd:(j + 1) * hd] = norm_rope(qkv[:, c0:c0 + hd], kg_ref[...]).astype(BF16)
    v_ref[...] = qkv[:, (nq + nk) * hd:].astype(BF16)


def _attn_qkv(xa, m, g, w, qg, kg, cosf, sinf, *, n_tiles, tpb, nb, qscale):
    r, d = xa.shape
    hd = qg.shape[1]
    nq, nk = ATT_HEADS * hd, ATT_KV_HEADS * hd
    tab_map = lambda i: (jnp.where(i < nb * tpb, i % tpb, tpb), 0)
    return pl.pallas_call(
        functools.partial(_attn_qkv_kernel, tpb=tpb, nb=nb, qscale=qscale),
        out_shape=(jax.ShapeDtypeStruct((r, nq), BF16), jax.ShapeDtypeStruct((r, nk), BF16),
                   jax.ShapeDtypeStruct((r, nk), BF16)),
        grid=(n_tiles,),
        in_specs=[pl.BlockSpec((ROW_TILE, d), lambda i: (i, 0)),
                  pl.BlockSpec(m.shape, lambda i: (0, 0)),
                  pl.BlockSpec((1, d), lambda i: (0, 0)),
                  pl.BlockSpec(w.shape, lambda i: (0, 0)),
                  pl.BlockSpec((1, hd), lambda i: (0, 0)),
                  pl.BlockSpec((1, hd), lambda i: (0, 0)),
                  pl.BlockSpec((ROW_TILE, hd), tab_map),
                  pl.BlockSpec((ROW_TILE, hd), tab_map)],
        out_specs=(pl.BlockSpec((ROW_TILE, nq), lambda i: (i, 0)),
                   pl.BlockSpec((ROW_TILE, nk), lambda i: (i, 0)),
                   pl.BlockSpec((ROW_TILE, nk), lambda i: (i, 0))),
        compiler_params=_cparams(("parallel",)),
        name="attn_qkv",
    )(xa, m, g, w, qg, kg, cosf, sinf)


def _flash_kernel(*refs, tq, tk, hd, n_lat_chunks):
    if n_lat_chunks:
        q_ref, kc_ref, vc_ref, kl_ref, vl_ref, o_ref, sa_ref, sb_ref = refs
    else:
        q_ref, kc_ref, vc_ref, o_ref = refs
    qs = jnp.concatenate([q_ref[:, g * hd:(g + 1) * hd] for g in range(ATT_GROUP)], axis=0)
    rows = ATT_GROUP * tq

    def scores(kblk):
        return lax.dot_general(qs, kblk, (((1,), (1,)), ((), ())), preferred_element_type=F32)

    def absorb(s, vblk, carry):
        m, l, acc = carry
        m_new = jnp.maximum(m, jnp.max(s, axis=-1, keepdims=True))
        p = jnp.exp2(s - m_new)
        alpha = jnp.exp2(m - m_new)
        l = alpha * l + jnp.sum(p, axis=-1, keepdims=True)
        acc = alpha * acc + jnp.dot(p.astype(BF16), vblk, preferred_element_type=F32)
        return m_new, l, acc

    def lat(ref, c):
        return ref[pl.ds(pl.multiple_of(c * tk, tk), tk), :]

    carry = (jnp.full((rows, 1), -jnp.inf, F32), jnp.zeros((rows, 1), F32), jnp.zeros((rows, hd), F32))
    carry = absorb(scores(kc_ref[...]), vc_ref[...], carry)
    def lat_scores(c):
        return jnp.dot(qs, kl_ref[c], preferred_element_type=F32)

    if n_lat_chunks == 1:
        carry = absorb(lat_scores(0), lat(vl_ref, 0), carry)
    elif n_lat_chunks:
        assert n_lat_chunks % 2 == 0
        sa_ref[...] = lat_scores(0)

        def body(c2, carry):
            c = 2 * c2
            sb_ref[...] = lat_scores(c + 1)
            carry = absorb(sa_ref[...], lat(vl_ref, c), carry)
            sa_ref[...] = lat_scores(jnp.minimum(c + 2, n_lat_chunks - 1))
            return absorb(sb_ref[...], lat(vl_ref, c + 1), carry)

        carry = lax.fori_loop(0, n_lat_chunks // 2, body, carry)
    _, l, acc = carry
    o = acc / l
    for g in range(ATT_GROUP):
        o_ref[:, g * hd:(g + 1) * hd] = o[g * tq:(g + 1) * tq].astype(BF16)


def _flash(q, k, v, *, nb, t_lat, n_ctx, hd, lat_queries):
    gw = ATT_GROUP * hd
    ctx_base = nb * t_lat // n_ctx
    kv_ctx = pl.BlockSpec((n_ctx, hd), lambda b, h, i: (ctx_base + b, h))
    if lat_queries:
        tq = min(256, t_lat)
        tk = min(1024, t_lat)
        nq_tiles = t_lat // tq
        q_spec = pl.BlockSpec((tq, gw), lambda b, h, i: (b * nq_tiles + i, h))
        n_chunks = t_lat // tk
        kv_lat = pl.BlockSpec((t_lat, hd), lambda b, h, i: (b, h))
        kt = k[:nb * t_lat].reshape(nb, n_chunks, tk, ATT_KV_HEADS, hd).transpose(0, 3, 1, 4, 2)
        kt_lat = pl.BlockSpec((None, None, n_chunks, hd, tk), lambda b, h, i: (b, h, 0, 0, 0))
        in_specs = [q_spec, kv_ctx, kv_ctx, kt_lat, kv_lat]
        args = (q, k, v, kt, v)
        out_rows = nb * t_lat
    else:
        tq = n_ctx
        tk = n_ctx
        nq_tiles = 1
        q_spec = pl.BlockSpec((tq, gw), lambda b, h, i: (ctx_base + b, h))
        in_specs = [q_spec, kv_ctx, kv_ctx]
        args = (q, k, v)
        n_chunks = 0
        out_rows = nb * n_ctx
    return pl.pallas_call(
        functools.partial(_flash_kernel, tq=tq, tk=tk, hd=hd, n_lat_chunks=n_chunks),
        out_shape=jax.ShapeDtypeStruct((out_rows, q.shape[1]), BF16),
        grid=(nb, ATT_KV_HEADS, nq_tiles),
        in_specs=in_specs,
        out_specs=pl.BlockSpec((tq, gw), lambda b, h, i: (b * nq_tiles + i, h)),
        scratch_shapes=[pltpu.VMEM((ATT_GROUP * tq, tk), F32)] * 2 if n_chunks else [],
        compiler_params=_cparams(("parallel", "parallel", "parallel")),
        name="flash_lat" if lat_queries else "flash_ctx",
    )(*args)


def _rope_tables(t_lat, hd):
    rows = t_lat // GRID_W
    r_idx, c_idx = jnp.meshgrid(jnp.arange(rows, dtype=F32), jnp.arange(GRID_W, dtype=F32), indexing='ij')
    pairs = hd // 4
    freqs = ROPE_THETA ** (-jnp.arange(pairs, dtype=F32) / pairs)
    ang = jnp.concatenate([r_idx.reshape(-1, 1) * freqs, c_idx.reshape(-1, 1) * freqs], axis=-1)
    cos, sin = jnp.cos(ang), jnp.sin(ang)
    cosf = jnp.concatenate([cos, cos], axis=-1)
    sinf = jnp.concatenate([-sin, sin], axis=-1)
    cosf = jnp.concatenate([cosf, jnp.ones((ROW_TILE, hd), F32)], axis=0)
    sinf = jnp.concatenate([sinf, jnp.zeros((ROW_TILE, hd), F32)], axis=0)
    return cosf, sinf


def _deinterleave_perm(hd):
    return np.concatenate([np.arange(0, hd, 2), np.arange(1, hd, 2)])


def _attention_layer(xa, m, g, w_in, q_gain, k_gain, w_out, geo, ctx_out):
    nb, t_lat, n_ctx, tpb, all_tiles, n_tiles = geo
    hd = q_gain.shape[0]
    perm = _deinterleave_perm(hd)
    nqk = (ATT_HEADS + ATT_KV_HEADS) * hd
    col = np.concatenate([(np.arange(nqk) // hd) * hd + perm[np.arange(nqk) % hd],
                          np.arange(nqk, w_in.shape[1])])
    w = w_in[:, col].astype(BF16)
    cosf, sinf = _rope_tables(t_lat, hd)
    qscale = hd ** -0.5 * math.log2(math.e)
    q, k, v = _attn_qkv(xa, m, g, w, q_gain[perm].reshape(1, hd), k_gain[perm].reshape(1, hd), cosf, sinf,
                        n_tiles=all_tiles, tpb=tpb, nb=nb, qscale=qscale)
    o = _flash(q, k, v, nb=nb, t_lat=t_lat, n_ctx=n_ctx, hd=hd, lat_queries=True)
    if ctx_out:
        o_c = _flash(q, k, v, nb=nb, t_lat=t_lat, n_ctx=n_ctx, hd=hd, lat_queries=False)
        o = jnp.concatenate([o, o_c], axis=0)
    return _resid_matmul(xa, o, w_out.astype(BF16), m, n_tiles=n_tiles, tpb=tpb, nb=nb, k_gate=2)


def _seq_block(b, c, *, nb, nc_ctx, nc_lat, reverse):
    ctx_base = nb * nc_lat
    in_ctx = c < nc_ctx
    cl = c - nc_ctx
    if reverse:
        return jnp.where(in_ctx, ctx_base + b * nc_ctx + (nc_ctx - 1 - c), b * nc_lat + (nc_lat - 1 - cl))
    return jnp.where(in_ctx, ctx_base + b * nc_ctx + c, b * nc_lat + cl)


def _roll_rows(x, s, reverse):
    return pltpu.roll(x, SUBLANES - s if reverse else s, 0)


def _last_row(x, reverse):
    return jnp.broadcast_to(x[0:1, :] if reverse else x[SUBLANES - 1:SUBLANES, :], x.shape)


def _s5_scan_kernel(u_ref, bd_ref, tab_ref, cd_ref, y_ref, hs_ref, st_ref, *, tc, ns, reverse):
    c = pl.program_id(1)

    @pl.when(c == 0)
    def _():
        st_ref[...] = jnp.zeros_like(st_ref)

    hs_ref[...] = jnp.dot(u_ref[...].astype(BF16), bd_ref[0], preferred_element_type=F32)
    ng = tc // SUBLANES

    def body(gi, carry):
        hr, hi = carry
        r0 = pl.multiple_of(((ng - 1 - gi) if reverse else gi) * SUBLANES, SUBLANES)
        br = hs_ref[pl.ds(r0, SUBLANES), 0:ns]
        bi = hs_ref[pl.ds(r0, SUBLANES), ns:2 * ns]
        for k, s in enumerate((1, 2, 4)):
            rr, ri = _roll_rows(br, s, reverse), _roll_rows(bi, s, reverse)
            mr, mi = tab_ref[0, 2 * k], tab_ref[0, 2 * k + 1]
            br, bi = br + mr * rr - mi * ri, bi + mr * ri + mi * rr
        pr, pi_ = tab_ref[0, 6], tab_ref[0, 7]
        nr = br + pr * hr - pi_ * hi
        ni = bi + pr * hi + pi_ * hr
        hs_ref[pl.ds(r0, SUBLANES), 0:ns] = nr
        hs_ref[pl.ds(r0, SUBLANES), ns:2 * ns] = ni
        return _last_row(nr, reverse), _last_row(ni, reverse)

    hr, hi = lax.fori_loop(0, ng, body, (st_ref[:, 0:ns], st_ref[:, ns:2 * ns]))
    st_ref[:, 0:ns] = hr
    st_ref[:, ns:2 * ns] = hi
    y_ref[...] = jnp.dot(hs_ref[...].astype(BF16), cd_ref[0], preferred_element_type=F32)


def _s5_scan(u, bd, tab, cd, geo, direction):
    nb, t_lat, n_ctx = geo[:3]
    r, width = u.shape
    ns = tab.shape[3]
    tc = SEQ_TILE
    nc_ctx, nc_lat = n_ctx // tc, t_lat // tc
    blk = functools.partial(_seq_block, nb=nb, nc_ctx=nc_ctx, nc_lat=nc_lat, reverse=direction == 1)
    return pl.pallas_call(
        functools.partial(_s5_scan_kernel, tc=tc, ns=ns, reverse=direction == 1),
        out_shape=jax.ShapeDtypeStruct((r, width), F32),
        grid=(nb, nc_ctx + nc_lat),
        in_specs=[pl.BlockSpec((tc, width), lambda b, c: (blk(b, c), 0)),
                  pl.BlockSpec((1, width, 2 * ns), lambda b, c: (direction, 0, 0)),
                  pl.BlockSpec((1, 8, SUBLANES, ns), lambda b, c: (direction, 0, 0, 0)),
                  pl.BlockSpec((1, 2 * ns, width), lambda b, c: (direction, 0, 0))],
        out_specs=pl.BlockSpec((tc, width), lambda b, c: (blk(b, c), 0)),
        scratch_shapes=[pltpu.VMEM((tc, 2 * ns), F32), pltpu.VMEM((SUBLANES, 2 * ns), F32)],
        compiler_params=_cparams(("arbitrary", "arbitrary")),
        name="s5_scan_bwd" if direction else "s5_scan_fwd",
    )(u, bd, tab, cd)


def _s5_out_kernel(x_ref, u_ref, yf_ref, yb_ref, dsk_ref, wg_ref, wo_ref, m_ref, o_ref, *, tpb, nb):
    d = x_ref.shape[1]
    sel = _row_sel(pl.program_id(0), tpb, nb)
    y = dsk_ref[...] * u_ref[...] + yf_ref[...] + yb_ref[...]
    z = _gelu(y)
    z = z * jax.nn.sigmoid(jnp.dot(z.astype(BF16), wg_ref[...], preferred_element_type=F32))
    out = jnp.dot(z.astype(BF16), wo_ref[...], preferred_element_type=F32)
    o_ref[...] = x_ref[...] + _mod_row(m_ref, sel, 2, d) * out


def _s5_out(xa, u, yf, yb, dsk, wg, wo, m, *, n_tiles, tpb, nb):
    r, d = xa.shape
    width = u.shape[1]
    return pl.pallas_call(
        functools.partial(_s5_out_kernel, tpb=tpb, nb=nb),
        out_shape=jax.ShapeDtypeStruct((r, d), F32),
        grid=(n_tiles,),
        in_specs=[pl.BlockSpec((ROW_TILE, d), lambda i: (i, 0)),
                  pl.BlockSpec((ROW_TILE, width), lambda i: (i, 0)),
                  pl.BlockSpec((ROW_TILE, width), lambda i: (i, 0)),
                  pl.BlockSpec((ROW_TILE, width), lambda i: (i, 0)),
                  pl.BlockSpec((1, width), lambda i: (0, 0)),
                  pl.BlockSpec(wg.shape, lambda i: (0, 0)),
                  pl.BlockSpec(wo.shape, lambda i: (0, 0)),
                  pl.BlockSpec(m.shape, lambda i: (0, 0))],
        out_specs=pl.BlockSpec((ROW_TILE, d), lambda i: (i, 0)),
        input_output_aliases={0: 0},
        compiler_params=_cparams(("parallel",)),
        name="s5_out",
    )(xa, u, yf, yb, dsk, wg, wo, m)


def _s5_layer(xa, m, g, w_in, lam_re, lam_im, log_dt, b_re, b_im, c_re, c_im, d_skip, w_glu, w_out, geo):
    nb, t_lat, n_ctx, tpb, all_tiles, n_tiles = geo
    groups, p_state = lam_re.shape[1], lam_re.shape[2]
    width = w_in.shape[1]
    ns = groups * p_state
    lam = lax.complex(lam_re.astype(F32), lam_im.astype(F32))
    dt = jnp.exp(log_dt.astype(F32))[..., None]
    lam_bar = jnp.exp(lam * dt)
    b_bar = ((lam_bar - 1.0) / lam)[..., None] * lax.complex(b_re.astype(F32), b_im.astype(F32))
    eye = jnp.eye(groups, dtype=F32)

    def block_diag_in(z):
        return jnp.einsum('dgpc,gh->dgchp', z, eye).reshape(2, groups * SSM_GROUP, ns)

    def block_diag_out(z):
        return jnp.einsum('dgcp,gh->dgphc', z, eye).reshape(2, ns, groups * SSM_GROUP)

    bd = jnp.concatenate([block_diag_in(jnp.real(b_bar)), block_diag_in(jnp.imag(b_bar))], axis=-1).astype(BF16)
    cd = jnp.concatenate([block_diag_out(c_re.astype(F32)), block_diag_out(-c_im.astype(F32))], axis=1).astype(BF16)
    pw = [lam_bar.reshape(2, ns)]
    for _ in range(SUBLANES - 1):
        pw.append(pw[-1] * pw[0])
    row = jnp.arange(SUBLANES)
    tabs = []
    for dirn in range(2):
        rev = dirn == 1
        planes = []
        for s in (1, 2, 4):
            keep = (row + s <= SUBLANES - 1) if rev else (row >= s)
            ms = jnp.where(keep[:, None], pw[s - 1][dirn][None, :], 0.0)
            planes += [jnp.real(ms), jnp.imag(ms)]
        pows = jnp.stack([pw[(SUBLANES - 1 - t) if rev else t][dirn] for t in range(SUBLANES)], axis=0)
        planes += [jnp.real(pows), jnp.imag(pows)]
        tabs.append(jnp.stack(planes, axis=0))
    tab = jnp.stack(tabs, axis=0).astype(F32)

    u = _norm_matmul(xa, m, g, w_in.astype(BF16), n_tiles=all_tiles, tpb=tpb, nb=nb, k_shift=0)
    yf = _s5_scan(u, bd, tab, cd, geo, 0)
    yb = _s5_scan(u, bd, tab, cd, geo, 1)
    return _s5_out(xa, u, yf, yb, d_skip.reshape(1, width).astype(F32), w_glu.astype(BF16), w_out.astype(BF16), m,
                   n_tiles=n_tiles, tpb=tpb, nb=nb)


def _lru_coef_kernel(cur_ref, prev_ref, next_ref, cw_ref, cb_ref, wrg_ref, brg_ref, wig_ref, big_ref, cn_ref,
                     a_ref, b_ref, buf_ref, *, tm, tps_lat, tps_ctx, n_lat_tiles):
    i = pl.program_id(0)
    j = i - n_lat_tiles
    first = jnp.where(i < n_lat_tiles, i % tps_lat == 0, j % tps_ctx == 0)
    last = jnp.where(i < n_lat_tiles, i % tps_lat == tps_lat - 1, j % tps_ctx == tps_ctx - 1)
    h = SUBLANES
    buf_ref[0:h, :] = jnp.where(first, 0.0, prev_ref[...])
    buf_ref[h:h + tm, :] = cur_ref[...]
    buf_ref[h + tm:h + tm + h, :] = jnp.where(last, 0.0, next_ref[...])
    xr = cb_ref[...] + sum(cw_ref[k:k + 1, :] * buf_ref[h - CONV_LEFT + k:h - CONV_LEFT + k + tm, :]
                           for k in range(CONV_W))
    nblk = wrg_ref.shape[1]
    bw = wrg_ref.shape[2]
    for n in range(nblk):
        sl = slice(n * bw, (n + 1) * bw)
        xs = xr[:, sl]
        xb = xs.astype(BF16)
        for d in range(2):
            r = jax.nn.sigmoid(jnp.dot(xb, wrg_ref[d, n], preferred_element_type=F32) + brg_ref[d, :, sl])
            ig = jax.nn.sigmoid(jnp.dot(xb, wig_ref[d, n], preferred_element_type=F32) + big_ref[d, :, sl])
            log_a = cn_ref[d, :, sl] * r
            a_ref[d, :, sl] = jnp.exp(log_a)
            b_ref[d, :, sl] = jnp.sqrt(1.0 - jnp.exp(2.0 * log_a)) * (ig * xs)


def _lru_coef(gx, conv_w, conv_b, wrg, brg, wig, big, cneg, geo):
    nb, t_lat, n_ctx = geo[:3]
    r = gx.shape[0]
    width = conv_w.shape[1]
    tm = SEQ_TILE
    hb = tm // SUBLANES
    n_lat_tiles = nb * t_lat // tm
    last_halo = r // SUBLANES - 1
    vec = lambda a: pl.BlockSpec(a.shape, lambda i: (0,) * a.ndim)
    return pl.pallas_call(
        functools.partial(_lru_coef_kernel, tm=tm, tps_lat=t_lat // tm, tps_ctx=n_ctx // tm, n_lat_tiles=n_lat_tiles),
        out_shape=(jax.ShapeDtypeStruct((2, r, width), F32), jax.ShapeDtypeStruct((2, r, width), F32)),
        grid=(r // tm,),
        in_specs=[pl.BlockSpec((tm, width), lambda i: (i, 1)),
                  pl.BlockSpec((SUBLANES, width), lambda i: (jnp.maximum(i * hb - 1, 0), 1)),
                  pl.BlockSpec((SUBLANES, width), lambda i: (jnp.minimum((i + 1) * hb, last_halo), 1)),
                  vec(conv_w), vec(conv_b), vec(wrg), vec(brg), vec(wig), vec(big), vec(cneg)],
        out_specs=(pl.BlockSpec((2, tm, width), lambda i: (0, i, 0)),
                   pl.BlockSpec((2, tm, width), lambda i: (0, i, 0))),
        scratch_shapes=[pltpu.VMEM((tm + 2 * SUBLANES, width), F32)],
        compiler_params=_cparams(("parallel",)),
        name="lru_coef",
    )(gx, gx, gx, conv_w, conv_b, wrg, brg, wig, big, cneg)


def _lru_scan_kernel(a_ref, b_ref, o_ref, st_ref, *, tc, reverse):
    c = pl.program_id(1)

    @pl.when(c == 0)
    def _():
        st_ref[...] = jnp.zeros_like(st_ref)

    width = a_ref.shape[-1]
    row = lax.broadcasted_iota(jnp.int32, (SUBLANES, width), 0)
    ng = tc // SUBLANES

    def body(gi, h):
        r0 = pl.multiple_of(((ng - 1 - gi) if reverse else gi) * SUBLANES, SUBLANES)
        a = a_ref[0, pl.ds(r0, SUBLANES), :]
        b = b_ref[0, pl.ds(r0, SUBLANES), :]
        for s in (1, 2, 4):
            keep = (row < SUBLANES - s) if reverse else (row >= s)
            ar = jnp.where(keep, _roll_rows(a, s, reverse), 1.0)
            br = jnp.where(keep, _roll_rows(b, s, reverse), 0.0)
            b = a * br + b
            a = a * ar
        hn = a * h + b
        o_ref[pl.ds(r0, SUBLANES), :] = hn
        return _last_row(hn, reverse)

    st_ref[...] = lax.fori_loop(0, ng, body, st_ref[...])


def _lru_scan(a, b, geo, direction):
    nb, t_lat, n_ctx = geo[:3]
    _, r, width = a.shape
    tc = SEQ_TILE
    nc_ctx, nc_lat = n_ctx // tc, t_lat // tc
    blk = functools.partial(_seq_block, nb=nb, nc_ctx=nc_ctx, nc_lat=nc_lat, reverse=direction == 1)
    spec = pl.BlockSpec((1, tc, width), lambda b_, c: (direction, blk(b_, c), 0))
    return pl.pallas_call(
        functools.partial(_lru_scan_kernel, tc=tc, reverse=direction == 1),
        out_shape=jax.ShapeDtypeStruct((r, width), F32),
        grid=(nb, nc_ctx + nc_lat),
        in_specs=[spec, spec],
        out_specs=pl.BlockSpec((tc, width), lambda b_, c: (blk(b_, c), 0)),
        scratch_shapes=[pltpu.VMEM((SUBLANES, width), F32)],
        compiler_params=_cparams(("arbitrary", "arbitrary")),
        name="lru_scan_bwd" if direction else "lru_scan_fwd",
    )(a, b)


def _lru_out_kernel(x_ref, gate_ref, sf_ref, sb_ref, wo_ref, m_ref, o_ref, *, tpb, nb):
    d = x_ref.shape[1]
    sel = _row_sel(pl.program_id(0), tpb, nb)
    y = (sf_ref[...] + sb_ref[...]) * _gelu(gate_ref[...])
    out = jnp.dot(y.astype(BF16), wo_ref[...], preferred_element_type=F32)
    o_ref[...] = x_ref[...] + _mod_row(m_ref, sel, 2, d) * out


def _lru_out(xa, gx, sf, sb, wo, m, *, n_tiles, tpb, nb):
    r, d = xa.shape
    width = sf.shape[1]
    return pl.pallas_call(
        functools.partial(_lru_out_kernel, tpb=tpb, nb=nb),
        out_shape=jax.ShapeDtypeStruct((r, d), F32),
        grid=(n_tiles,),
        in_specs=[pl.BlockSpec((ROW_TILE, d), lambda i: (i, 0)),
                  pl.BlockSpec((ROW_TILE, width), lambda i: (i, 0)),
                  pl.BlockSpec((ROW_TILE, width), lambda i: (i, 0)),
                  pl.BlockSpec((ROW_TILE, width), lambda i: (i, 0)),
                  pl.BlockSpec(wo.shape, lambda i: (0, 0)),
                  pl.BlockSpec(m.shape, lambda i: (0, 0))],
        out_specs=pl.BlockSpec((ROW_TILE, d), lambda i: (i, 0)),
        input_output_aliases={0: 0},
        compiler_params=_cparams(("parallel",)),
        name="lru_out",
    )(xa, gx, sf, sb, wo, m)


def _lru_layer(xa, m, g, w_in, conv_w, conv_b, w_rg, b_rg, w_ig, b_ig, lam, w_out, geo):
    nb, t_lat, n_ctx, tpb, all_tiles, n_tiles = geo
    width = conv_w.shape[1]
    gx = _norm_matmul(xa, m, g, w_in.astype(BF16), n_tiles=all_tiles, tpb=tpb, nb=nb, k_shift=0)
    cneg = (-LRU_C * jax.nn.softplus(-lam.astype(F32))).reshape(2, 1, width)
    a, b = _lru_coef(gx, conv_w.astype(F32), conv_b.reshape(1, width).astype(F32),
                     w_rg.astype(BF16), b_rg.reshape(2, 1, width).astype(F32),
                     w_ig.astype(BF16), b_ig.reshape(2, 1, width).astype(F32), cneg, geo)
    sf = _lru_scan(a, b, geo, 0)
    sb = _lru_scan(a, b, geo, 1)
    return _lru_out(xa, gx, sf, sb, w_out.astype(BF16), m, n_tiles=n_tiles, tpb=tpb, nb=nb)


def _oddeven_merge_sort_pairs(n):
    pairs = []
    p = 1
    while p < n:
        k = p
        while k >= 1:
            for j in range(k % p, n - k, 2 * k):
                for i in range(min(k, n - j - k)):
                    if (i + j) // (2 * p) == (i + j + k) // (2 * p):
                        pairs.append((i + j, i + j + k))
            k //= 2
        p *= 2
    return pairs


_SORT16 = _oddeven_merge_sort_pairs(PEER_TOPK)


def _ce(v, i, j):
    a, b = v[i], v[j]
    v[i] = jnp.maximum(a, b)
    v[j] = jnp.minimum(a, b)


def _bitonic_merge_desc(v):
    n = len(v)
    k = n // 2
    while k >= 1:
        for i in range(n):
            if i & k == 0:
                _ce(v, i, i | k)
        k //= 2


def _top16_desc(blk):
    n = PEER_TOPK
    v = [blk[SUBLANES * a:SUBLANES * (a + 1), :] for a in range(n)]
    for (i, j) in _SORT16:
        _ce(v, i, j)
    for shift in (4, 2, 1):
        o = [pltpu.roll(x, shift, 0) for x in v]
        v = [jnp.maximum(v[i], o[n - 1 - i]) for i in range(n)]
        _bitonic_merge_desc(v)
    return v


_KEY_FLOOR = 2.0 ** -100


def _peer_route(s, nk):
    nh, k = PEER_HEADS, PEER_TOPK
    w = s.shape[1]
    low = nk - 1
    key = lax.broadcasted_iota(jnp.int32, (nk, w), 0)
    sub = lax.broadcasted_iota(jnp.int32, (SUBLANES, w), 0)
    vals, idxs = [], []
    for half in range(2):
        pk = [jnp.zeros((SUBLANES, w), F32) for _ in range(k)]
        for hh in range(nh):
            blk = s[(half * nh + hh) * nk:(half * nh + hh + 1) * nk, :]
            bits = lax.bitcast_convert_type(jnp.where(jnp.abs(blk) < _KEY_FLOOR, _KEY_FLOOR, blk), jnp.int32)
            neg = (bits >> 31) & low
            keys = lax.bitcast_convert_type((bits & ~low) | ((low - key) ^ neg), F32)
            top = _top16_desc(keys)
            pk = [jnp.where(sub == hh, top[i], pk[i]) for i in range(k)]
        pkb = [lax.bitcast_convert_type(x, jnp.int32) for x in pk]
        idxs.append([low - ((x & low) ^ ((x >> 31) & low)) for x in pkb])
        vals.append([lax.bitcast_convert_type(x & ~low, F32) for x in pkb])
    pairs = [(i, j) for i in range(k) for j in range(k) if (i + 1) * (j + 1) <= k]
    cur = [vals[0][i] + vals[1][j] for (i, j) in pairs]
    best, codes = [], []
    for _ in range(k):
        mx = functools.reduce(jnp.maximum, cur)
        hc = functools.reduce(jnp.minimum, [jnp.where(c == mx, i * k + j, k * k) for c, (i, j) in zip(cur, pairs)])
        cur = [jnp.where(hc == i * k + j, -jnp.inf, c) for c, (i, j) in zip(cur, pairs)]
        best.append(mx)
        codes.append(hc)
    ex = [jnp.exp(b - best[0]) for b in best]
    zinv = 1.0 / functools.reduce(lambda a, b: a + b, ex)
    i1, i2 = [], []
    for hc in codes:
        ri, rj = hc >> (k.bit_length() - 1), hc & (k - 1)
        i1.append(functools.reduce(lambda a, b: a + b, [jnp.where(ri == i, idxs[0][i], 0) for i in range(k)]))
        i2.append(functools.reduce(lambda a, b: a + b, [jnp.where(rj == j, idxs[1][j], 0) for j in range(k)]))
    cat = lambda xs: jnp.concatenate(xs, axis=0)
    return cat(i1), cat(i2), cat([e * zinv for e in ex])


def _peer_score_kernel(x_ref, m_ref, g_ref, wq_ref, kk_ref, h_ref, i1_ref, i2_ref, gate_ref, s_ref, *, tpb, nb):
    d = x_ref.shape[1]
    nk = kk_ref.shape[1]
    kd = kk_ref.shape[2]
    sel = _row_sel(pl.program_id(0), tpb, nb)
    h = _modnorm(x_ref[...], g_ref[...], _mod_row(m_ref, sel, 3, d), _mod_row(m_ref, sel, 4, d)).astype(BF16)
    h_ref[...] = h
    q = jnp.dot(h, wq_ref[...], preferred_element_type=F32).astype(BF16)
    for half in range(2):
        for hh in range(PEER_HEADS):
            jq = hh * 2 + half
            blk = half * PEER_HEADS + hh
            s_ref[blk * nk:(blk + 1) * nk, :] = lax.dot_general(
                kk_ref[blk], q[:, jq * kd:(jq + 1) * kd], (((1,), (1,)), ((), ())),
                preferred_element_type=F32)
    i1, i2, gate = _peer_route(s_ref, nk)
    i1_ref[...] = i1.astype(F32).T
    i2_ref[...] = i2.astype(F32).T
    gate_ref[...] = gate.T


def _peer_scores(xa, m, g, wq, kk, *, n_tiles, tpb, nb):
    r, d = xa.shape
    nk = kk.shape[1]
    slots = PEER_HEADS * PEER_TOPK
    slot_spec = pl.BlockSpec((ROW_TILE, slots), lambda i: (i, 0))
    slot_shape = jax.ShapeDtypeStruct((r, slots), F32)
    return pl.pallas_call(
        functools.partial(_peer_score_kernel, tpb=tpb, nb=nb),
        out_shape=(jax.ShapeDtypeStruct((r, d), BF16), slot_shape, slot_shape, slot_shape),
        grid=(n_tiles,),
        in_specs=[pl.BlockSpec((ROW_TILE, d), lambda i: (i, 0)),
                  pl.BlockSpec(m.shape, lambda i: (0, 0)),
                  pl.BlockSpec((1, d), lambda i: (0, 0)),
                  pl.BlockSpec(wq.shape, lambda i: (0, 0)),
                  pl.BlockSpec(kk.shape, lambda i: (0, 0, 0))],
        out_specs=(pl.BlockSpec((ROW_TILE, d), lambda i: (i, 0)), slot_spec, slot_spec, slot_spec),
        scratch_shapes=[pltpu.VMEM((2 * PEER_HEADS * nk, ROW_TILE), F32)],
        compiler_params=_cparams(("parallel",)),
        name="peer_scores",
    )(xa, m, g, wq, kk)


PEER_EXPERT_BLOCK = 2048


def _peer_dense_kernel(x_ref, h_ref, i1_ref, i2_ref, gate_ref, ut_ref, v_ref, m_ref, o_ref,
                       acc_ref, w_ref, *, tpb, nb, nk, pitch):
    d = x_ref.shape[1]
    tm = x_ref.shape[0]
    eb = ut_ref.shape[1]
    n_i1 = eb // nk
    rows = nk // 2
    steps = rows // n_i1
    j = pl.program_id(1)
    hi_mask = jnp.int32(-65536)

    @pl.when(j == 0)
    def _():
        acc_ref[...] = jnp.zeros_like(acc_ref)
        slots = i1_ref.shape[1]
        key = lax.broadcasted_iota(jnp.int32, (nk, slots), 0).astype(F32)

        def chunk(ci, carry):
            r0 = pl.multiple_of(ci * SUBLANES, SUBLANES)
            i1c = i1_ref[pl.ds(r0, SUBLANES), :]
            i2c = i2_ref[pl.ds(r0, SUBLANES), :]
            gc = gate_ref[pl.ds(r0, SUBLANES), :]
            for r in range(SUBLANES):
                lhs = jnp.where(key == i1c[r:r + 1, :], gc[r:r + 1, :], 0.0).astype(BF16)
                rhs = jnp.where(key == i2c[r:r + 1, :], 1.0, 0.0).astype(BF16)
                wt = lax.dot_general(lhs, rhs, (((1,), (1,)), ((), ())), preferred_element_type=F32)
                bits = lax.bitcast_convert_type(wt, jnp.int32) + 0x8000
                word = (bits[rows:] & hi_mask) | lax.shift_right_logical(bits[:rows], 16)
                w_ref[pl.ds(pl.multiple_of((r0 + r) * pitch, SUBLANES), rows), :] = word
            return carry

        lax.fori_loop(0, tm // SUBLANES, chunk, 0, unroll=8)

    s = jnp.dot(h_ref[...], ut_ref[...], preferred_element_type=F32)
    jj = j % steps
    shift = jnp.where(j < steps, 16, 0)
    p = []
    for ii in range(n_i1):
        word = w_ref[pl.ds(jj * n_i1 + ii, tm, stride=pitch), :]
        wb = lax.bitcast_convert_type(lax.shift_left(word, jnp.broadcast_to(shift, word.shape)) & hi_mask, F32)
        p.append((wb * _gelu(s[:, ii * nk:(ii + 1) * nk])).astype(BF16))
    acc_ref[...] += jnp.dot(jnp.concatenate(p, axis=1), v_ref[...], preferred_element_type=F32)

    @pl.when(j == pl.num_programs(1) - 1)
    def _():
        sel = _row_sel(pl.program_id(0), tpb, nb)
        o_ref[...] = x_ref[...] + _mod_row(m_ref, sel, 5, d) * acc_ref[...]


def _peer_dense(xa, h, i1, i2, gate, ut, v, m, *, n_tiles, tpb, nb, nk):
    r, d = xa.shape
    n_exp = v.shape[0]
    eb = PEER_EXPERT_BLOCK
    tm = ROW_TILE
    slots = i1.shape[1]
    rows = nk // 2
    assert rows % (eb // nk) == 0 and n_exp == nk * nk
    pitch = rows + SUBLANES if (rows // SUBLANES) % 2 == 0 else rows
    slot_spec = pl.BlockSpec((tm, slots), lambda i, j: (i, 0))
    return pl.pallas_call(
        functools.partial(_peer_dense_kernel, tpb=tpb, nb=nb, nk=nk, pitch=pitch),
        out_shape=jax.ShapeDtypeStruct((r, d), F32),
        grid=(n_tiles, n_exp // eb),
        in_specs=[pl.BlockSpec((tm, d), lambda i, j: (i, 0)),
                  pl.BlockSpec((tm, d), lambda i, j: (i, 0)),
                  slot_spec, slot_spec, slot_spec,
                  pl.BlockSpec((d, eb), lambda i, j: (0, j)),
                  pl.BlockSpec((eb, d), lambda i, j: (j, 0)),
                  pl.BlockSpec(m.shape, lambda i, j: (0, 0))],
        out_specs=pl.BlockSpec((tm, d), lambda i, j: (i, 0)),
        scratch_shapes=[pltpu.VMEM((tm, d), F32),
                        pltpu.VMEM((tm * pitch, nk), jnp.int32)],
        input_output_aliases={0: 0},
        compiler_params=pltpu.CompilerParams(dimension_semantics=("parallel", "arbitrary"),
                                             vmem_limit_bytes=PEER_VMEM_LIMIT),
        name="peer_dense",
    )(xa, h, i1, i2, gate, ut, v, m)


def _peer_layer(xa, m, g, w_q, k1, k2, u_tab, v_tab, geo):
    nb, t_lat, n_ctx, tpb, all_tiles, n_tiles = geo
    nk = k1.shape[1]
    kk = jnp.concatenate([k1, k2], axis=0).astype(BF16)
    h, i1, i2, gate = _peer_scores(xa, m, g, w_q.astype(BF16), kk, n_tiles=n_tiles, tpb=tpb, nb=nb)
    return _peer_dense(xa, h, i1, i2, gate, u_tab.astype(BF16).T, v_tab.astype(BF16), m,
                       n_tiles=n_tiles, tpb=tpb, nb=nb, nk=nk)


def _final_norm_kernel(x_ref, g_ref, o_ref):
    x = x_ref[...]
    o_ref[...] = x * lax.rsqrt(jnp.mean(x * x, axis=-1, keepdims=True) + EPS) * g_ref[...]


def _final_norm(xa, g, n_rows):
    d = xa.shape[1]
    return pl.pallas_call(
        _final_norm_kernel,
        out_shape=jax.ShapeDtypeStruct((n_rows, d), F32),
        grid=(n_rows // ROW_TILE,),
        in_specs=[pl.BlockSpec((ROW_TILE, d), lambda i: (i, 0)), pl.BlockSpec((1, d), lambda i: (0, 0))],
        out_specs=pl.BlockSpec((ROW_TILE, d), lambda i: (i, 0)),
        compiler_params=_cparams(("parallel",)),
        name="final_norm",
    )(xa, g)


def kernel(x, c, ctx, c_ctx, mod_w, mod_b, norm_mix, norm_ffn, norm_final, attn_w_in, attn_q_gain, attn_k_gain, attn_w_out, ssm_w_in, ssm_lam_re, ssm_lam_im, ssm_log_dt, ssm_b_re, ssm_b_im, ssm_c_re, ssm_c_im, ssm_d, ssm_w_glu, ssm_w_out, lru_w_in, lru_conv_w, lru_conv_b, lru_w_rg, lru_b_rg, lru_w_ig, lru_b_ig, lru_lam, lru_w_out, peer_w_q, peer_k1, peer_k2, peer_u, peer_v):
    nb, t_lat, d = x.shape
    n_ctx = ctx.shape[1]
    depth = mod_w.shape[0]
    assert t_lat % ROW_TILE == 0 and (nb * n_ctx) % ROW_TILE == 0 and n_ctx % SEQ_TILE == 0
    assert nb + 1 <= SUBLANES and t_lat % GRID_W == 0
    tpb = t_lat // ROW_TILE
    lat_tiles = nb * tpb
    all_tiles = lat_tiles + nb * n_ctx // ROW_TILE

    xa = jnp.concatenate([x.reshape(nb * t_lat, d), ctx.reshape(nb * n_ctx, d)], axis=0).astype(F32)
    crows = jnp.concatenate([c.astype(F32), c_ctx.reshape(1, d).astype(F32),
                             jnp.zeros((SUBLANES - nb - 1, d), F32)], axis=0)
    mods = _modulation(crows, mod_w.astype(F32), mod_b.astype(F32))

    for i in range(depth):
        ctx_out = i < depth - 1
        geo = (nb, t_lat, n_ctx, tpb, all_tiles, all_tiles if ctx_out else lat_tiles)
        kind, j = i % N_MIXERS, i // N_MIXERS
        m = mods[i]
        g_mix = norm_mix[i].reshape(1, d).astype(F32)
        if kind == 0:
            xa = _attention_layer(xa, m, g_mix, attn_w_in[j], attn_q_gain[j], attn_k_gain[j], attn_w_out[j],
                                  geo, ctx_out)
        elif kind == 1:
            xa = _s5_layer(xa, m, g_mix, ssm_w_in[j], ssm_lam_re[j], ssm_lam_im[j], ssm_log_dt[j],
                           ssm_b_re[j], ssm_b_im[j], ssm_c_re[j], ssm_c_im[j], ssm_d[j],
                           ssm_w_glu[j], ssm_w_out[j], geo)
        else:
            xa = _lru_layer(xa, m, g_mix, lru_w_in[j], lru_conv_w[j], lru_conv_b[j], lru_w_rg[j], lru_b_rg[j],
                            lru_w_ig[j], lru_b_ig[j], lru_lam[j], lru_w_out[j], geo)
        xa = _peer_layer(xa, m, norm_ffn[i].reshape(1, d).astype(F32), peer_w_q[i], peer_k1[i], peer_k2[i],
                         peer_u[i], peer_v[i], geo)
    out = _final_norm(xa, norm_final.reshape(1, d).astype(F32), nb * t_lat)
    return out.reshape(nb, t_lat, d)
```

```python
import functools
import math

import numpy as np
import jax
import jax.numpy as jnp
from jax import lax
from jax.experimental import pallas as pl
from jax.experimental.pallas import tpu as pltpu

F32 = jnp.float32
BF16 = jnp.bfloat16

EPS = 1e-6
GRID_W = 64
ROPE_THETA = 10000.0
ATT_HEADS = 8
ATT_KV_HEADS = 2
ATT_GROUP = ATT_HEADS // ATT_KV_HEADS
SSM_GROUP = 16
LRU_BLOCKS = 8
CONV_W = 4
CONV_LEFT = 2
LRU_C = 8.0
PEER_HEADS = 8
PEER_TOPK = 16
N_MIXERS = 3

LANES = 128
SUBLANES = 8
VMEM_LIMIT = 48 * 1024 * 1024
PEER_VMEM_LIMIT = 56 * 1024 * 1024

ROW_TILE = 512
SEQ_TILE = 256


def _cparams(sem):
    return pltpu.CompilerParams(dimension_semantics=sem, vmem_limit_bytes=VMEM_LIMIT)


def _gelu(x):
    return 0.5 * x * (1.0 + lax.erf(x * (1.0 / math.sqrt(2.0))))


def _modnorm(x, g, shift, scale):
    y = x * lax.rsqrt(jnp.mean(x * x, axis=-1, keepdims=True) + EPS) * g
    return y * (1.0 + scale) + shift


def _mod_row(m_ref, sel, k, d):
    return m_ref[pl.ds(sel, 1), k * d:(k + 1) * d]


def _row_sel(i, tiles_per_batch, n_batch):
    return jnp.minimum(i // tiles_per_batch, n_batch)


def _mod_kernel(c_ref, w_ref, b_ref, o_ref):
    c = c_ref[...]
    s = c * jax.nn.sigmoid(c)
    o_ref[0] = jnp.dot(s.astype(BF16), w_ref[0].astype(BF16), preferred_element_type=F32) + b_ref[0]


def _modulation(crows, mod_w, mod_b):
    depth, d, n = mod_w.shape
    tn = 1536
    return pl.pallas_call(
        _mod_kernel,
        out_shape=jax.ShapeDtypeStruct((depth, SUBLANES, n), F32),
        grid=(depth, n // tn),
        in_specs=[pl.BlockSpec((SUBLANES, d), lambda l, j: (0, 0)),
                  pl.BlockSpec((1, d, tn), lambda l, j: (l, 0, j)),
                  pl.BlockSpec((1, 1, tn), lambda l, j: (l, 0, j))],
        out_specs=pl.BlockSpec((1, SUBLANES, tn), lambda l, j: (l, 0, j)),
        compiler_params=_cparams(("parallel", "parallel")),
        name="modulation",
    )(crows, mod_w, mod_b.reshape(depth, 1, n))


def _norm_matmul_kernel(x_ref, m_ref, g_ref, w_ref, o_ref, *, tpb, nb, k_shift):
    d = x_ref.shape[1]
    sel = _row_sel(pl.program_id(0), tpb, nb)
    h = _modnorm(x_ref[...], g_ref[...], _mod_row(m_ref, sel, k_shift, d), _mod_row(m_ref, sel, k_shift + 1, d))
    o_ref[...] = jnp.dot(h.astype(BF16), w_ref[...], preferred_element_type=F32).astype(o_ref.dtype)


def _norm_matmul(xa, m, g, w, *, n_tiles, tpb, nb, k_shift, out_dtype=F32):
    r, d = xa.shape
    n = w.shape[1]
    return pl.pallas_call(
        functools.partial(_norm_matmul_kernel, tpb=tpb, nb=nb, k_shift=k_shift),
        out_shape=jax.ShapeDtypeStruct((r, n), out_dtype),
        grid=(n_tiles,),
        in_specs=[pl.BlockSpec((ROW_TILE, d), lambda i: (i, 0)),
                  pl.BlockSpec(m.shape, lambda i: (0, 0)),
                  pl.BlockSpec((1, d), lambda i: (0, 0)),
                  pl.BlockSpec(w.shape, lambda i: (0, 0))],
        out_specs=pl.BlockSpec((ROW_TILE, n), lambda i: (i, 0)),
        compiler_params=_cparams(("parallel",)),
        name="norm_matmul",
    )(xa, m, g, w)


def _resid_matmul_kernel(x_ref, a_ref, w_ref, m_ref, o_ref, *, tpb, nb, k_gate):
    d = x_ref.shape[1]
    sel = _row_sel(pl.program_id(0), tpb, nb)
    y = jnp.dot(a_ref[...], w_ref[...], preferred_element_type=F32)
    o_ref[...] = x_ref[...] + _mod_row(m_ref, sel, k_gate, d) * y


def _resid_matmul(xa, a, w, m, *, n_tiles, tpb, nb, k_gate):
    r, d = xa.shape
    return pl.pallas_call(
        functools.partial(_resid_matmul_kernel, tpb=tpb, nb=nb, k_gate=k_gate),
        out_shape=jax.ShapeDtypeStruct((r, d), F32),
        grid=(n_tiles,),
        in_specs=[pl.BlockSpec((ROW_TILE, d), lambda i: (i, 0)),
                  pl.BlockSpec((ROW_TILE, a.shape[1]), lambda i: (i, 0)),
                  pl.BlockSpec(w.shape, lambda i: (0, 0)),
                  pl.BlockSpec(m.shape, lambda i: (0, 0))],
        out_specs=pl.BlockSpec((ROW_TILE, d), lambda i: (i, 0)),
        input_output_aliases={0: 0},
        compiler_params=_cparams(("parallel",)),
        name="resid_matmul",
    )(xa, a, w, m)


def _attn_qkv_kernel(x_ref, m_ref, g_ref, w_ref, qg_ref, kg_ref, cos_ref, sin_ref,
                     q_ref, k_ref, v_ref, *, tpb, nb, qscale):
    d = x_ref.shape[1]
    hd = qg_ref.shape[1]
    sel = _row_sel(pl.program_id(0), tpb, nb)
    h = _modnorm(x_ref[...], g_ref[...], _mod_row(m_ref, sel, 0, d), _mod_row(m_ref, sel, 1, d))
    qkv = jnp.dot(h.astype(BF16), w_ref[...], preferred_element_type=F32)
    cosf = cos_ref[...]
    sinf = sin_ref[...]

    def norm_rope(z, gain):
        zn = z * lax.rsqrt(jnp.mean(z * z, axis=-1, keepdims=True) + EPS) * gain
        return zn * cosf + pltpu.roll(zn, hd // 2, 1) * sinf

    nq = q_ref.shape[1] // hd
    nk = k_ref.shape[1] // hd
    for j in range(nq):
        q_ref[:, j * hd:(j + 1) * hd] = (norm_rope(qkv[:, j * hd:(j + 1) * hd], qg_ref[...]) * qscale).astype(BF16)
    for j in range(nk):
        c0 = (nq + j) * hd
        k_ref[:, j * hd:(j + 1) * hd] = norm_rope(qkv[:, c0:c0 + hd], kg_ref[...]).astype(BF16)
    v_ref[...] = qkv[:, (nq + nk) * hd:].astype(BF16)


def _attn_qkv(xa, m, g, w, qg, kg, cosf, sinf, *, n_tiles, tpb, nb, qscale):
    r, d = xa.shape
    hd = qg.shape[1]
    nq, nk = ATT_HEADS * hd, ATT_KV_HEADS * hd
    tab_map = lambda i: (jnp.where(i < nb * tpb, i % tpb, tpb), 0)
    return pl.pallas_call(
        functools.partial(_attn_qkv_kernel, tpb=tpb, nb=nb, qscale=qscale),
        out_shape=(jax.ShapeDtypeStruct((r, nq), BF16), jax.ShapeDtypeStruct((r, nk), BF16),
                   jax.ShapeDtypeStruct((r, nk), BF16)),
        grid=(n_tiles,),
        in_specs=[pl.BlockSpec((ROW_TILE, d), lambda i: (i, 0)),
                  pl.BlockSpec(m.shape, lambda i: (0, 0)),
                  pl.BlockSpec((1, d), lambda i: (0, 0)),
                  pl.BlockSpec(w.shape, lambda i: (0, 0)),
                  pl.BlockSpec((1, hd), lambda i: (0, 0)),
                  pl.BlockSpec((1, hd), lambda i: (0, 0)),
                  pl.BlockSpec((ROW_TILE, hd), tab_map),
                  pl.BlockSpec((ROW_TILE, hd), tab_map)],
        out_specs=(pl.BlockSpec((ROW_TILE, nq), lambda i: (i, 0)),
                   pl.BlockSpec((ROW_TILE, nk), lambda i: (i, 0)),
                   pl.BlockSpec((ROW_TILE, nk), lambda i: (i, 0))),
        compiler_params=_cparams(("parallel",)),
        name="attn_qkv",
    )(xa, m, g, w, qg, kg, cosf, sinf)


def _flash_kernel(*refs, tq, tk, hd, n_lat_chunks):
    if n_lat_chunks:
        q_ref, kc_ref, vc_ref, kl_ref, vl_ref, o_ref, sa_ref, sb_ref = refs
    else:
        q_ref, kc_ref, vc_ref, o_ref = refs
    qs = jnp.concatenate([q_ref[:, g * hd:(g + 1) * hd] for g in range(ATT_GROUP)], axis=0)
    rows = ATT_GROUP * tq

    def scores(kblk):
        return lax.dot_general(qs, kblk, (((1,), (1,)), ((), ())), preferred_element_type=F32)

    def absorb(s, vblk, carry):
        m, l, acc = carry
        m_new = jnp.maximum(m, jnp.max(s, axis=-1, keepdims=True))
        p = jnp.exp2((s - m_new).astype(BF16))
        alpha = jnp.exp2(m - m_new)
        psum = functools.reduce(jnp.add, [p[:, i * LANES:(i + 1) * LANES] for i in range(p.shape[1] // LANES)])
        l = alpha * l + jnp.sum(psum.astype(F32), axis=-1, keepdims=True)
        acc = alpha * acc + jnp.dot(p, vblk, preferred_element_type=F32)
        return m_new, l, acc

    def lat(ref, c):
        return ref[pl.ds(pl.multiple_of(c * tk, tk), tk), :]

    carry = (jnp.full((rows, 1), -jnp.inf, F32), jnp.zeros((rows, 1), F32), jnp.zeros((rows, hd), F32))
    carry = absorb(scores(kc_ref[...]), vc_ref[...], carry)
    def lat_scores(c):
        return jnp.dot(qs, kl_ref[c], preferred_element_type=F32)

    if n_lat_chunks == 1:
        carry = absorb(lat_scores(0), lat(vl_ref, 0), carry)
    elif n_lat_chunks:
        assert n_lat_chunks % 2 == 0
        sa_ref[...] = lat_scores(0)

        def body(c2, carry):
            c = 2 * c2
            sb_ref[...] = lat_scores(c + 1)
            carry = absorb(sa_ref[...], lat(vl_ref, c), carry)
            sa_ref[...] = lat_scores(jnp.minimum(c + 2, n_lat_chunks - 1))
            return absorb(sb_ref[...], lat(vl_ref, c + 1), carry)

        carry = lax.fori_loop(0, n_lat_chunks // 2, body, carry, unroll=4 if n_lat_chunks % 8 == 0 else 1)
    _, l, acc = carry
    o = acc / l
    for g in range(ATT_GROUP):
        o_ref[:, g * hd:(g + 1) * hd] = o[g * tq:(g + 1) * tq].astype(BF16)


def _flash(q, k, v, *, nb, t_lat, n_ctx, hd, lat_queries):
    gw = ATT_GROUP * hd
    ctx_base = nb * t_lat // n_ctx
    kv_ctx = pl.BlockSpec((n_ctx, hd), lambda b, h, i: (ctx_base + b, h))
    if lat_queries:
        tq = min(256, t_lat)
        tk = min(1024, t_lat)
        nq_tiles = t_lat // tq
        q_spec = pl.BlockSpec((tq, gw), lambda b, h, i: (b * nq_tiles + i, h))
        n_chunks = t_lat // tk
        kv_lat = pl.BlockSpec((t_lat, hd), lambda b, h, i: (b, h))
        kt = k[:nb * t_lat].reshape(nb, n_chunks, tk, ATT_KV_HEADS, hd).transpose(0, 3, 1, 4, 2)
        kt_lat = pl.BlockSpec((None, None, n_chunks, hd, tk), lambda b, h, i: (b, h, 0, 0, 0))
        in_specs = [q_spec, kv_ctx, kv_ctx, kt_lat, kv_lat]
        args = (q, k, v, kt, v)
        out_rows = nb * t_lat
    else:
        tq = n_ctx
        tk = n_ctx
        nq_tiles = 1
        q_spec = pl.BlockSpec((tq, gw), lambda b, h, i: (ctx_base + b, h))
        in_specs = [q_spec, kv_ctx, kv_ctx]
        args = (q, k, v)
        n_chunks = 0
        out_rows = nb * n_ctx
    return pl.pallas_call(
        functools.partial(_flash_kernel, tq=tq, tk=tk, hd=hd, n_lat_chunks=n_chunks),
        out_shape=jax.ShapeDtypeStruct((out_rows, q.shape[1]), BF16),
        grid=(nb, ATT_KV_HEADS, nq_tiles),
        in_specs=in_specs,
        out_specs=pl.BlockSpec((tq, gw), lambda b, h, i: (b * nq_tiles + i, h)),
        scratch_shapes=[pltpu.VMEM((ATT_GROUP * tq, tk), F32)] * 2 if n_chunks else [],
        compiler_params=_cparams(("parallel", "parallel", "parallel")),
        name="flash_lat" if lat_queries else "flash_ctx",
    )(*args)


def _rope_tables(t_lat, hd):
    rows = t_lat // GRID_W
    r_idx, c_idx = jnp.meshgrid(jnp.arange(rows, dtype=F32), jnp.arange(GRID_W, dtype=F32), indexing='ij')
    pairs = hd // 4
    freqs = ROPE_THETA ** (-jnp.arange(pairs, dtype=F32) / pairs)
    ang = jnp.concatenate([r_idx.reshape(-1, 1) * freqs, c_idx.reshape(-1, 1) * freqs], axis=-1)
    cos, sin = jnp.cos(ang), jnp.sin(ang)
    cosf = jnp.concatenate([cos, cos], axis=-1)
    sinf = jnp.concatenate([-sin, sin], axis=-1)
    cosf = jnp.concatenate([cosf, jnp.ones((ROW_TILE, hd), F32)], axis=0)
    sinf = jnp.concatenate([sinf, jnp.zeros((ROW_TILE, hd), F32)], axis=0)
    return cosf, sinf


def _deinterleave_perm(hd):
    return np.concatenate([np.arange(0, hd, 2), np.arange(1, hd, 2)])


def _attention_layer(xa, m, g, w_in, q_gain, k_gain, w_out, geo, ctx_out):
    nb, t_lat, n_ctx, tpb, all_tiles, n_tiles = geo
    hd = q_gain.shape[0]
    perm = _deinterleave_perm(hd)
    nqk = (ATT_HEADS + ATT_KV_HEADS) * hd
    col = np.concatenate([(np.arange(nqk) // hd) * hd + perm[np.arange(nqk) % hd],
                          np.arange(nqk, w_in.shape[1])])
    w = w_in[:, col].astype(BF16)
    cosf, sinf = _rope_tables(t_lat, hd)
    qscale = hd ** -0.5 * math.log2(math.e)
    q, k, v = _attn_qkv(xa, m, g, w, q_gain[perm].reshape(1, hd), k_gain[perm].reshape(1, hd), cosf, sinf,
                        n_tiles=all_tiles, tpb=tpb, nb=nb, qscale=qscale)
    o = _flash(q, k, v, nb=nb, t_lat=t_lat, n_ctx=n_ctx, hd=hd, lat_queries=True)
    if ctx_out:
        o_c = _flash(q, k, v, nb=nb, t_lat=t_lat, n_ctx=n_ctx, hd=hd, lat_queries=False)
        o = jnp.concatenate([o, o_c], axis=0)
    return _resid_matmul(xa, o, w_out.astype(BF16), m, n_tiles=n_tiles, tpb=tpb, nb=nb, k_gate=2)


def _seq_block(b, c, *, nb, nc_ctx, nc_lat, reverse):
    ctx_base = nb * nc_lat
    in_ctx = c < nc_ctx
    cl = c - nc_ctx
    if reverse:
        return jnp.where(in_ctx, ctx_base + b * nc_ctx + (nc_ctx - 1 - c), b * nc_lat + (nc_lat - 1 - cl))
    return jnp.where(in_ctx, ctx_base + b * nc_ctx + c, b * nc_lat + cl)


def _roll_rows(x, s, reverse):
    return pltpu.roll(x, SUBLANES - s if reverse else s, 0)


def _last_row(x, reverse):
    return jnp.broadcast_to(x[0:1, :] if reverse else x[SUBLANES - 1:SUBLANES, :], x.shape)


def _s5_scan_kernel(u_ref, bd_ref, tab_ref, cd_ref, y_ref, hs_ref, st_ref, *, tc, ns, reverse):
    c = pl.program_id(1)

    @pl.when(c == 0)
    def _():
        st_ref[...] = jnp.zeros_like(st_ref)

    wh, sh = u_ref.shape[1] // 2, ns // 2
    for g in range(2):
        ub = u_ref[:, g * wh:(g + 1) * wh].astype(BF16)
        for part in range(2):
            cols = slice(part * ns + g * sh, part * ns + (g + 1) * sh)
            hs_ref[:, cols] = jnp.dot(ub, bd_ref[0, g * wh:(g + 1) * wh, cols], preferred_element_type=F32)
    ng = tc // SUBLANES

    def body(gi, carry):
        hr, hi = carry
        r0 = pl.multiple_of(((ng - 1 - gi) if reverse else gi) * SUBLANES, SUBLANES)
        br = hs_ref[pl.ds(r0, SUBLANES), 0:ns]
        bi = hs_ref[pl.ds(r0, SUBLANES), ns:2 * ns]
        for k, s in enumerate((1, 2, 4)):
            rr, ri = _roll_rows(br, s, reverse), _roll_rows(bi, s, reverse)
            mr, mi = tab_ref[0, 2 * k], tab_ref[0, 2 * k + 1]
            br, bi = br + mr * rr - mi * ri, bi + mr * ri + mi * rr
        pr, pi_ = tab_ref[0, 6], tab_ref[0, 7]
        nr = br + pr * hr - pi_ * hi
        ni = bi + pr * hi + pi_ * hr
        hs_ref[pl.ds(r0, SUBLANES), 0:ns] = nr
        hs_ref[pl.ds(r0, SUBLANES), ns:2 * ns] = ni
        return _last_row(nr, reverse), _last_row(ni, reverse)

    hr, hi = lax.fori_loop(0, ng, body, (st_ref[:, 0:ns], st_ref[:, ns:2 * ns]))
    st_ref[:, 0:ns] = hr
    st_ref[:, ns:2 * ns] = hi
    for g in range(2):
        y_ref[:, g * wh:(g + 1) * wh] = sum(
            jnp.dot(hs_ref[:, part * ns + g * sh:part * ns + (g + 1) * sh].astype(BF16),
                    cd_ref[0, part * ns + g * sh:part * ns + (g + 1) * sh, g * wh:(g + 1) * wh],
                    preferred_element_type=F32) for part in range(2))


def _s5_scan(u, bd, tab, cd, geo, direction):
    nb, t_lat, n_ctx = geo[:3]
    r, width = u.shape
    ns = tab.shape[3]
    tc = SEQ_TILE
    nc_ctx, nc_lat = n_ctx // tc, t_lat // tc
    blk = functools.partial(_seq_block, nb=nb, nc_ctx=nc_ctx, nc_lat=nc_lat, reverse=direction == 1)
    return pl.pallas_call(
        functools.partial(_s5_scan_kernel, tc=tc, ns=ns, reverse=direction == 1),
        out_shape=jax.ShapeDtypeStruct((r, width), F32),
        grid=(nb, nc_ctx + nc_lat),
        in_specs=[pl.BlockSpec((tc, width), lambda b, c: (blk(b, c), 0)),
                  pl.BlockSpec((1, width, 2 * ns), lambda b, c: (direction, 0, 0)),
                  pl.BlockSpec((1, 8, SUBLANES, ns), lambda b, c: (direction, 0, 0, 0)),
                  pl.BlockSpec((1, 2 * ns, width), lambda b, c: (direction, 0, 0))],
        out_specs=pl.BlockSpec((tc, width), lambda b, c: (blk(b, c), 0)),
        scratch_shapes=[pltpu.VMEM((tc, 2 * ns), F32), pltpu.VMEM((SUBLANES, 2 * ns), F32)],
        compiler_params=_cparams(("arbitrary", "arbitrary")),
        name="s5_scan_bwd" if direction else "s5_scan_fwd",
    )(u, bd, tab, cd)


def _s5_out_kernel(x_ref, u_ref, yf_ref, yb_ref, dsk_ref, wg_ref, wo_ref, m_ref, o_ref, *, tpb, nb):
    d = x_ref.shape[1]
    sel = _row_sel(pl.program_id(0), tpb, nb)
    y = dsk_ref[...] * u_ref[...] + yf_ref[...] + yb_ref[...]
    z = _gelu(y)
    z = z * jax.nn.sigmoid(jnp.dot(z.astype(BF16), wg_ref[...], preferred_element_type=F32))
    out = jnp.dot(z.astype(BF16), wo_ref[...], preferred_element_type=F32)
    o_ref[...] = x_ref[...] + _mod_row(m_ref, sel, 2, d) * out


def _s5_out(xa, u, yf, yb, dsk, wg, wo, m, *, n_tiles, tpb, nb):
    r, d = xa.shape
    width = u.shape[1]
    return pl.pallas_call(
        functools.partial(_s5_out_kernel, tpb=tpb, nb=nb),
        out_shape=jax.ShapeDtypeStruct((r, d), F32),
        grid=(n_tiles,),
        in_specs=[pl.BlockSpec((ROW_TILE, d), lambda i: (i, 0)),
                  pl.BlockSpec((ROW_TILE, width), lambda i: (i, 0)),
                  pl.BlockSpec((ROW_TILE, width), lambda i: (i, 0)),
                  pl.BlockSpec((ROW_TILE, width), lambda i: (i, 0)),
                  pl.BlockSpec((1, width), lambda i: (0, 0)),
                  pl.BlockSpec(wg.shape, lambda i: (0, 0)),
                  pl.BlockSpec(wo.shape, lambda i: (0, 0)),
                  pl.BlockSpec(m.shape, lambda i: (0, 0))],
        out_specs=pl.BlockSpec((ROW_TILE, d), lambda i: (i, 0)),
        input_output_aliases={0: 0},
        compiler_params=_cparams(("parallel",)),
        name="s5_out",
    )(xa, u, yf, yb, dsk, wg, wo, m)


def _s5_layer(xa, m, g, w_in, lam_re, lam_im, log_dt, b_re, b_im, c_re, c_im, d_skip, w_glu, w_out, geo):
    nb, t_lat, n_ctx, tpb, all_tiles, n_tiles = geo
    groups, p_state = lam_re.shape[1], lam_re.shape[2]
    width = w_in.shape[1]
    ns = groups * p_state
    lam = lax.complex(lam_re.astype(F32), lam_im.astype(F32))
    dt = jnp.exp(log_dt.astype(F32))[..., None]
    lam_bar = jnp.exp(lam * dt)
    b_bar = ((lam_bar - 1.0) / lam)[..., None] * lax.complex(b_re.astype(F32), b_im.astype(F32))
    eye = jnp.eye(groups, dtype=F32)

    def block_diag_in(z):
        return jnp.einsum('dgpc,gh->dgchp', z, eye).reshape(2, groups * SSM_GROUP, ns)

    def block_diag_out(z):
        return jnp.einsum('dgcp,gh->dgphc', z, eye).reshape(2, ns, groups * SSM_GROUP)

    bd = jnp.concatenate([block_diag_in(jnp.real(b_bar)), block_diag_in(jnp.imag(b_bar))], axis=-1).astype(BF16)
    cd = jnp.concatenate([block_diag_out(c_re.astype(F32)), block_diag_out(-c_im.astype(F32))], axis=1).astype(BF16)
    pw = [lam_bar.reshape(2, ns)]
    for _ in range(SUBLANES - 1):
        pw.append(pw[-1] * pw[0])
    row = jnp.arange(SUBLANES)
    tabs = []
    for dirn in range(2):
        rev = dirn == 1
        planes = []
        for s in (1, 2, 4):
            keep = (row + s <= SUBLANES - 1) if rev else (row >= s)
            ms = jnp.where(keep[:, None], pw[s - 1][dirn][None, :], 0.0)
            planes += [jnp.real(ms), jnp.imag(ms)]
        pows = jnp.stack([pw[(SUBLANES - 1 - t) if rev else t][dirn] for t in range(SUBLANES)], axis=0)
        planes += [jnp.real(pows), jnp.imag(pows)]
        tabs.append(jnp.stack(planes, axis=0))
    tab = jnp.stack(tabs, axis=0).astype(F32)

    u = _norm_matmul(xa, m, g, w_in.astype(BF16), n_tiles=all_tiles, tpb=tpb, nb=nb, k_shift=0)
    yf = _s5_scan(u, bd, tab, cd, geo, 0)
    yb = _s5_scan(u, bd, tab, cd, geo, 1)
    return _s5_out(xa, u, yf, yb, d_skip.reshape(1, width).astype(F32), w_glu.astype(BF16), w_out.astype(BF16), m,
                   n_tiles=n_tiles, tpb=tpb, nb=nb)


def _lru_coef_kernel(cur_ref, prev_ref, next_ref, cw_ref, cb_ref, wrg_ref, brg_ref, wig_ref, big_ref, cn_ref,
                     a_ref, b_ref, buf_ref, *, tm, tps_lat, tps_ctx, n_lat_tiles):
    i = pl.program_id(0)
    j = i - n_lat_tiles
    first = jnp.where(i < n_lat_tiles, i % tps_lat == 0, j % tps_ctx == 0)
    last = jnp.where(i < n_lat_tiles, i % tps_lat == tps_lat - 1, j % tps_ctx == tps_ctx - 1)
    h = SUBLANES
    buf_ref[0:h, :] = jnp.where(first, 0.0, prev_ref[...])
    buf_ref[h:h + tm, :] = cur_ref[...]
    buf_ref[h + tm:h + tm + h, :] = jnp.where(last, 0.0, next_ref[...])
    xr = cb_ref[...] + sum(cw_ref[k:k + 1, :] * buf_ref[h - CONV_LEFT + k:h - CONV_LEFT + k + tm, :]
                           for k in range(CONV_W))
    nblk = wrg_ref.shape[1]
    bw = wrg_ref.shape[2]
    for n in range(nblk):
        sl = slice(n * bw, (n + 1) * bw)
        xs = xr[:, sl]
        xb = xs.astype(BF16)
        for d in range(2):
            r = jax.nn.sigmoid(jnp.dot(xb, wrg_ref[d, n], preferred_element_type=F32) + brg_ref[d, :, sl])
            ig = jax.nn.sigmoid(jnp.dot(xb, wig_ref[d, n], preferred_element_type=F32) + big_ref[d, :, sl])
            log_a = cn_ref[d, :, sl] * r
            a_ref[d, :, sl] = jnp.exp(log_a)
            b_ref[d, :, sl] = jnp.sqrt(1.0 - jnp.exp(2.0 * log_a)) * (ig * xs)


def _lru_coef(gx, conv_w, conv_b, wrg, brg, wig, big, cneg, geo):
    nb, t_lat, n_ctx = geo[:3]
    r = gx.shape[0]
    width = conv_w.shape[1]
    tm = SEQ_TILE
    hb = tm // SUBLANES
    n_lat_tiles = nb * t_lat // tm
    last_halo = r // SUBLANES - 1
    vec = lambda a: pl.BlockSpec(a.shape, lambda i: (0,) * a.ndim)
    return pl.pallas_call(
        functools.partial(_lru_coef_kernel, tm=tm, tps_lat=t_lat // tm, tps_ctx=n_ctx // tm, n_lat_tiles=n_lat_tiles),
        out_shape=(jax.ShapeDtypeStruct((2, r, width), F32), jax.ShapeDtypeStruct((2, r, width), F32)),
        grid=(r // tm,),
        in_specs=[pl.BlockSpec((tm, width), lambda i: (i, 1)),
                  pl.BlockSpec((SUBLANES, width), lambda i: (jnp.maximum(i * hb - 1, 0), 1)),
                  pl.BlockSpec((SUBLANES, width), lambda i: (jnp.minimum((i + 1) * hb, last_halo), 1)),
                  vec(conv_w), vec(conv_b), vec(wrg), vec(brg), vec(wig), vec(big), vec(cneg)],
        out_specs=(pl.BlockSpec((2, tm, width), lambda i: (0, i, 0)),
                   pl.BlockSpec((2, tm, width), lambda i: (0, i, 0))),
        scratch_shapes=[pltpu.VMEM((tm + 2 * SUBLANES, width), F32)],
        compiler_params=_cparams(("parallel",)),
        name="lru_coef",
    )(gx, gx, gx, conv_w, conv_b, wrg, brg, wig, big, cneg)


def _lru_scan_kernel(a_ref, b_ref, o_ref, st_ref, *, tc, reverse):
    c = pl.program_id(1)

    @pl.when(c == 0)
    def _():
        st_ref[...] = jnp.zeros_like(st_ref)

    width = a_ref.shape[-1]
    row = lax.broadcasted_iota(jnp.int32, (SUBLANES, width), 0)
    ng = tc // SUBLANES

    def body(gi, h):
        r0 = pl.multiple_of(((ng - 1 - gi) if reverse else gi) * SUBLANES, SUBLANES)
        a = a_ref[0, pl.ds(r0, SUBLANES), :]
        b = b_ref[0, pl.ds(r0, SUBLANES), :]
        for s in (1, 2, 4):
            keep = (row < SUBLANES - s) if reverse else (row >= s)
            ar = jnp.where(keep, _roll_rows(a, s, reverse), 1.0)
            br = jnp.where(keep, _roll_rows(b, s, reverse), 0.0)
            b = a * br + b
            a = a * ar
        hn = a * h + b
        o_ref[pl.ds(r0, SUBLANES), :] = hn
        return _last_row(hn, reverse)

    st_ref[...] = lax.fori_loop(0, ng, body, st_ref[...])


def _lru_scan(a, b, geo, direction):
    nb, t_lat, n_ctx = geo[:3]
    _, r, width = a.shape
    tc = SEQ_TILE
    nc_ctx, nc_lat = n_ctx // tc, t_lat // tc
    blk = functools.partial(_seq_block, nb=nb, nc_ctx=nc_ctx, nc_lat=nc_lat, reverse=direction == 1)
    spec = pl.BlockSpec((1, tc, width), lambda b_, c: (direction, blk(b_, c), 0))
    return pl.pallas_call(
        functools.partial(_lru_scan_kernel, tc=tc, reverse=direction == 1),
        out_shape=jax.ShapeDtypeStruct((r, width), F32),
        grid=(nb, nc_ctx + nc_lat),
        in_specs=[spec, spec],
        out_specs=pl.BlockSpec((tc, width), lambda b_, c: (blk(b_, c), 0)),
        scratch_shapes=[pltpu.VMEM((SUBLANES, width), F32)],
        compiler_params=_cparams(("arbitrary", "arbitrary")),
        name="lru_scan_bwd" if direction else "lru_scan_fwd",
    )(a, b)


def _lru_out_kernel(x_ref, gate_ref, sf_ref, sb_ref, wo_ref, m_ref, o_ref, *, tpb, nb):
    d = x_ref.shape[1]
    sel = _row_sel(pl.program_id(0), tpb, nb)
    y = (sf_ref[...] + sb_ref[...]) * _gelu(gate_ref[...])
    out = jnp.dot(y.astype(BF16), wo_ref[...], preferred_element_type=F32)
    o_ref[...] = x_ref[...] + _mod_row(m_ref, sel, 2, d) * out


def _lru_out(xa, gx, sf, sb, wo, m, *, n_tiles, tpb, nb):
    r, d = xa.shape
    width = sf.shape[1]
    return pl.pallas_call(
        functools.partial(_lru_out_kernel, tpb=tpb, nb=nb),
        out_shape=jax.ShapeDtypeStruct((r, d), F32),
        grid=(n_tiles,),
        in_specs=[pl.BlockSpec((ROW_TILE, d), lambda i: (i, 0)),
                  pl.BlockSpec((ROW_TILE, width), lambda i: (i, 0)),
                  pl.BlockSpec((ROW_TILE, width), lambda i: (i, 0)),
                  pl.BlockSpec((ROW_TILE, width), lambda i: (i, 0)),
                  pl.BlockSpec(wo.shape, lambda i: (0, 0)),
                  pl.BlockSpec(m.shape, lambda i: (0, 0))],
        out_specs=pl.BlockSpec((ROW_TILE, d), lambda i: (i, 0)),
        input_output_aliases={0: 0},
        compiler_params=_cparams(("parallel",)),
        name="lru_out",
    )(xa, gx, sf, sb, wo, m)


def _lru_layer(xa, m, g, w_in, conv_w, conv_b, w_rg, b_rg, w_ig, b_ig, lam, w_out, geo):
    nb, t_lat, n_ctx, tpb, all_tiles, n_tiles = geo
    width = conv_w.shape[1]
    gx = _norm_matmul(xa, m, g, w_in.astype(BF16), n_tiles=all_tiles, tpb=tpb, nb=nb, k_shift=0)
    cneg = (-LRU_C * jax.nn.softplus(-lam.astype(F32))).reshape(2, 1, width)
    a, b = _lru_coef(gx, conv_w.astype(F32), conv_b.reshape(1, width).astype(F32),
                     w_rg.astype(BF16), b_rg.reshape(2, 1, width).astype(F32),
                     w_ig.astype(BF16), b_ig.reshape(2, 1, width).astype(F32), cneg, geo)
    sf = _lru_scan(a, b, geo, 0)
    sb = _lru_scan(a, b, geo, 1)
    return _lru_out(xa, gx, sf, sb, w_out.astype(BF16), m, n_tiles=n_tiles, tpb=tpb, nb=nb)


def _oddeven_merge_sort_pairs(n):
    pairs = []
    p = 1
    while p < n:
        k = p
        while k >= 1:
            for j in range(k % p, n - k, 2 * k):
                for i in range(min(k, n - j - k)):
                    if (i + j) // (2 * p) == (i + j + k) // (2 * p):
                        pairs.append((i + j, i + j + k))
            k //= 2
        p *= 2
    return pairs


_SORT16 = _oddeven_merge_sort_pairs(PEER_TOPK)


def _ce(v, i, j):
    a, b = v[i], v[j]
    v[i] = jnp.maximum(a, b)
    v[j] = jnp.minimum(a, b)


def _bitonic_merge_desc(v):
    n = len(v)
    k = n // 2
    while k >= 1:
        for i in range(n):
            if i & k == 0:
                _ce(v, i, i | k)
        k //= 2


def _top16_desc(blk):
    n = PEER_TOPK
    v = [blk[SUBLANES * a:SUBLANES * (a + 1), :] for a in range(n)]
    for (i, j) in _SORT16:
        _ce(v, i, j)
    for shift in (4, 2, 1):
        o = [pltpu.roll(x, shift, 0) for x in v]
        v = [jnp.maximum(v[i], o[n - 1 - i]) for i in range(n)]
        _bitonic_merge_desc(v)
    return v


_KEY_FLOOR = 2.0 ** -100


def _peer_route(s, nk):
    nh, k = PEER_HEADS, PEER_TOPK
    w = s.shape[1]
    low = nk - 1
    key = lax.broadcasted_iota(jnp.int32, (nk, w), 0)
    sub = lax.broadcasted_iota(jnp.int32, (SUBLANES, w), 0)
    vals, idxs = [], []
    for half in range(2):
        pk = [jnp.zeros((SUBLANES, w), F32) for _ in range(k)]
        for hh in range(nh):
            blk = s[(half * nh + hh) * nk:(half * nh + hh + 1) * nk, :]
            bits = lax.bitcast_convert_type(jnp.where(jnp.abs(blk) < _KEY_FLOOR, _KEY_FLOOR, blk), jnp.int32)
            neg = (bits >> 31) & low
            keys = lax.bitcast_convert_type((bits & ~low) | ((low - key) ^ neg), F32)
            top = _top16_desc(keys)
            pk = [jnp.where(sub == hh, top[i], pk[i]) for i in range(k)]
        pkb = [lax.bitcast_convert_type(x, jnp.int32) for x in pk]
        idxs.append([low - ((x & low) ^ ((x >> 31) & low)) for x in pkb])
        vals.append([lax.bitcast_convert_type(x & ~low, F32) for x in pkb])
    pairs = [(i, j) for i in range(k) for j in range(k) if (i + 1) * (j + 1) <= k]
    cur = [vals[0][i] + vals[1][j] for (i, j) in pairs]
    best, codes = [], []
    for _ in range(k):
        mx = functools.reduce(jnp.maximum, cur)
        hc = functools.reduce(jnp.minimum, [jnp.where(c == mx, i * k + j, k * k) for c, (i, j) in zip(cur, pairs)])
        cur = [jnp.where(hc == i * k + j, -jnp.inf, c) for c, (i, j) in zip(cur, pairs)]
        best.append(mx)
        codes.append(hc)
    ex = [jnp.exp(b - best[0]) for b in best]
    zinv = 1.0 / functools.reduce(lambda a, b: a + b, ex)
    i1, i2 = [], []
    for hc in codes:
        ri, rj = hc >> (k.bit_length() - 1), hc & (k - 1)
        i1.append(functools.reduce(lambda a, b: a + b, [jnp.where(ri == i, idxs[0][i], 0) for i in range(k)]))
        i2.append(functools.reduce(lambda a, b: a + b, [jnp.where(rj == j, idxs[1][j], 0) for j in range(k)]))
    cat = lambda xs: jnp.concatenate(xs, axis=0)
    return cat(i1), cat(i2), cat([e * zinv for e in ex])


def _peer_score_kernel(x_ref, m_ref, g_ref, wq_ref, kk_ref, h_ref, i1_ref, i2_ref, gate_ref, s_ref, *, tpb, nb):
    d = x_ref.shape[1]
    nk = kk_ref.shape[1]
    kd = kk_ref.shape[2]
    sel = _row_sel(pl.program_id(0), tpb, nb)
    h = _modnorm(x_ref[...], g_ref[...], _mod_row(m_ref, sel, 3, d), _mod_row(m_ref, sel, 4, d)).astype(BF16)
    h_ref[...] = h
    q = jnp.dot(h, wq_ref[...], preferred_element_type=F32).astype(BF16)
    for half in range(2):
        for hh in range(PEER_HEADS):
            jq = hh * 2 + half
            blk = half * PEER_HEADS + hh
            s_ref[blk * nk:(blk + 1) * nk, :] = lax.dot_general(
                kk_ref[blk], q[:, jq * kd:(jq + 1) * kd], (((1,), (1,)), ((), ())),
                preferred_element_type=F32)
    i1, i2, gate = _peer_route(s_ref, nk)
    i1_ref[...] = i1.astype(F32).T
    i2_ref[...] = i2.astype(F32).T
    gate_ref[...] = gate.T


def _peer_scores(xa, m, g, wq, kk, *, n_tiles, tpb, nb):
    r, d = xa.shape
    nk = kk.shape[1]
    slots = PEER_HEADS * PEER_TOPK
    slot_spec = pl.BlockSpec((ROW_TILE, slots), lambda i: (i, 0))
    slot_shape = jax.ShapeDtypeStruct((r, slots), F32)
    return pl.pallas_call(
        functools.partial(_peer_score_kernel, tpb=tpb, nb=nb),
        out_shape=(jax.ShapeDtypeStruct((r, d), BF16), slot_shape, slot_shape, slot_shape),
        grid=(n_tiles,),
        in_specs=[pl.BlockSpec((ROW_TILE, d), lambda i: (i, 0)),
                  pl.BlockSpec(m.shape, lambda i: (0, 0)),
                  pl.BlockSpec((1, d), lambda i: (0, 0)),
                  pl.BlockSpec(wq.shape, lambda i: (0, 0)),
                  pl.BlockSpec(kk.shape, lambda i: (0, 0, 0))],
        out_specs=(pl.BlockSpec((ROW_TILE, d), lambda i: (i, 0)), slot_spec, slot_spec, slot_spec),
        scratch_shapes=[pltpu.VMEM((2 * PEER_HEADS * nk, ROW_TILE), F32)],
        compiler_params=_cparams(("parallel",)),
        name="peer_scores",
    )(xa, m, g, wq, kk)


PEER_EXPERT_BLOCK = 2048


def _peer_dense_kernel(x_ref, h_ref, i1_ref, i2_ref, gate_ref, ut_ref, v_ref, m_ref, o_ref,
                       acc_ref, w_ref, *, tpb, nb, nk, pitch):
    d = x_ref.shape[1]
    tm = x_ref.shape[0]
    eb = ut_ref.shape[1]
    n_i1 = eb // nk
    rows = nk // 2
    steps = rows // n_i1
    j = pl.program_id(1)
    hi_mask = jnp.int32(-65536)

    @pl.when(j == 0)
    def _():
        acc_ref[...] = jnp.zeros_like(acc_ref)
        slots = i1_ref.shape[1]
        key = lax.broadcasted_iota(jnp.int32, (nk, slots), 0).astype(F32)

        def chunk(ci, carry):
            r0 = pl.multiple_of(ci * SUBLANES, SUBLANES)
            i1c = i1_ref[pl.ds(r0, SUBLANES), :]
            i2c = i2_ref[pl.ds(r0, SUBLANES), :]
            gc = gate_ref[pl.ds(r0, SUBLANES), :]
            for r in range(SUBLANES):
                lhs = jnp.where(key == i1c[r:r + 1, :], gc[r:r + 1, :], 0.0).astype(BF16)
                rhs = jnp.where(key == i2c[r:r + 1, :], 1.0, 0.0).astype(BF16)
                wt = lax.dot_general(lhs, rhs, (((1,), (1,)), ((), ())), preferred_element_type=F32)
                bits = lax.bitcast_convert_type(wt, jnp.int32) + 0x8000
                word = (bits[rows:] & hi_mask) | lax.shift_right_logical(bits[:rows], 16)
                w_ref[pl.ds(pl.multiple_of((r0 + r) * pitch, SUBLANES), rows), :] = word
            return carry

        lax.fori_loop(0, tm // SUBLANES, chunk, 0, unroll=8)

    s = jnp.dot(h_ref[...], ut_ref[...], preferred_element_type=F32)
    jj = j % steps
    shift = jnp.where(j < steps, 16, 0)
    p = []
    for ii in range(n_i1):
        word = w_ref[pl.ds(jj * n_i1 + ii, tm, stride=pitch), :]
        wb = lax.bitcast_convert_type(lax.shift_left(word, jnp.broadcast_to(shift, word.shape)) & hi_mask, F32)
        p.append((wb * _gelu(s[:, ii * nk:(ii + 1) * nk])).astype(BF16))
    acc_ref[...] += jnp.dot(jnp.concatenate(p, axis=1), v_ref[...], preferred_element_type=F32)

    @pl.when(j == pl.num_programs(1) - 1)
    def _():
        sel = _row_sel(pl.program_id(0), tpb, nb)
        o_ref[...] = x_ref[...] + _mod_row(m_ref, sel, 5, d) * acc_ref[...]


def _peer_dense(xa, h, i1, i2, gate, ut, v, m, *, n_tiles, tpb, nb, nk):
    r, d = xa.shape
    n_exp = v.shape[0]
    eb = PEER_EXPERT_BLOCK
    tm = ROW_TILE
    slots = i1.shape[1]
    rows = nk // 2
    assert rows % (eb // nk) == 0 and n_exp == nk * nk
    pitch = rows + SUBLANES if (rows // SUBLANES) % 2 == 0 else rows
    slot_spec = pl.BlockSpec((tm, slots), lambda i, j: (i, 0))
    return pl.pallas_call(
        functools.partial(_peer_dense_kernel, tpb=tpb, nb=nb, nk=nk, pitch=pitch),
        out_shape=jax.ShapeDtypeStruct((r, d), F32),
        grid=(n_tiles, n_exp // eb),
        in_specs=[pl.BlockSpec((tm, d), lambda i, j: (i, 0)),
                  pl.BlockSpec((tm, d), lambda i, j: (i, 0)),
                  slot_spec, slot_spec, slot_spec,
                  pl.BlockSpec((d, eb), lambda i, j: (0, j)),
                  pl.BlockSpec((eb, d), lambda i, j: (j, 0)),
                  pl.BlockSpec(m.shape, lambda i, j: (0, 0))],
        out_specs=pl.BlockSpec((tm, d), lambda i, j: (i, 0)),
        scratch_shapes=[pltpu.VMEM((tm, d), F32),
                        pltpu.VMEM((tm * pitch, nk), jnp.int32)],
        input_output_aliases={0: 0},
        compiler_params=pltpu.CompilerParams(dimension_semantics=("parallel", "arbitrary"),
                                             vmem_limit_bytes=PEER_VMEM_LIMIT),
        name="peer_dense",
    )(xa, h, i1, i2, gate, ut, v, m)


def _peer_layer(xa, m, g, w_q, k1, k2, u_tab, v_tab, geo):
    nb, t_lat, n_ctx, tpb, all_tiles, n_tiles = geo
    nk = k1.shape[1]
    kk = jnp.concatenate([k1, k2], axis=0).astype(BF16)
    h, i1, i2, gate = _peer_scores(xa, m, g, w_q.astype(BF16), kk, n_tiles=n_tiles, tpb=tpb, nb=nb)
    return _peer_dense(xa, h, i1, i2, gate, u_tab.astype(BF16).T, v_tab.astype(BF16), m,
                       n_tiles=n_tiles, tpb=tpb, nb=nb, nk=nk)


def _final_norm_kernel(x_ref, g_ref, o_ref):
    x = x_ref[...]
    o_ref[...] = x * lax.rsqrt(jnp.mean(x * x, axis=-1, keepdims=True) + EPS) * g_ref[...]


def _final_norm(xa, g, n_rows):
    d = xa.shape[1]
    return pl.pallas_call(
        _final_norm_kernel,
        out_shape=jax.ShapeDtypeStruct((n_rows, d), F32),
        grid=(n_rows // ROW_TILE,),
        in_specs=[pl.BlockSpec((ROW_TILE, d), lambda i: (i, 0)), pl.BlockSpec((1, d), lambda i: (0, 0))],
        out_specs=pl.BlockSpec((ROW_TILE, d), lambda i: (i, 0)),
        compiler_params=_cparams(("parallel",)),
        name="final_norm",
    )(xa, g)


def kernel(x, c, ctx, c_ctx, mod_w, mod_b, norm_mix, norm_ffn, norm_final, attn_w_in, attn_q_gain, attn_k_gain, attn_w_out, ssm_w_in, ssm_lam_re, ssm_lam_im, ssm_log_dt, ssm_b_re, ssm_b_im, ssm_c_re, ssm_c_im, ssm_d, ssm_w_glu, ssm_w_out, lru_w_in, lru_conv_w, lru_conv_b, lru_w_rg, lru_b_rg, lru_w_ig, lru_b_ig, lru_lam, lru_w_out, peer_w_q, peer_k1, peer_k2, peer_u, peer_v):
    nb, t_lat, d = x.shape
    n_ctx = ctx.shape[1]
    depth = mod_w.shape[0]
    assert t_lat % ROW_TILE == 0 and (nb * n_ctx) % ROW_TILE == 0 and n_ctx % SEQ_TILE == 0
    assert nb + 1 <= SUBLANES and t_lat % GRID_W == 0
    tpb = t_lat // ROW_TILE
    lat_tiles = nb * tpb
    all_tiles = lat_tiles + nb * n_ctx // ROW_TILE

    xa = jnp.concatenate([x.reshape(nb * t_lat, d), ctx.reshape(nb * n_ctx, d)], axis=0).astype(F32)
    crows = jnp.concatenate([c.astype(F32), c_ctx.reshape(1, d).astype(F32),
                             jnp.zeros((SUBLANES - nb - 1, d), F32)], axis=0)
    mods = _modulation(crows, mod_w.astype(F32), mod_b.astype(F32))

    for i in range(depth):
        ctx_out = i < depth - 1
        geo = (nb, t_lat, n_ctx, tpb, all_tiles, all_tiles if ctx_out else lat_tiles)
        kind, j = i % N_MIXERS, i // N_MIXERS
        m = mods[i]
        g_mix = norm_mix[i].reshape(1, d).astype(F32)
        if kind == 0:
            xa = _attention_layer(xa, m, g_mix, attn_w_in[j], attn_q_gain[j], attn_k_gain[j], attn_w_out[j],
                                  geo, ctx_out)
        elif kind == 1:
            xa = _s5_layer(xa, m, g_mix, ssm_w_in[j], ssm_lam_re[j], ssm_lam_im[j], ssm_log_dt[j],
                           ssm_b_re[j], ssm_b_im[j], ssm_c_re[j], ssm_c_im[j], ssm_d[j],
                           ssm_w_glu[j], ssm_w_out[j], geo)
        else:
            xa = _lru_layer(xa, m, g_mix, lru_w_in[j], lru_conv_w[j], lru_conv_b[j], lru_w_rg[j], lru_b_rg[j],
                            lru_w_ig[j], lru_b_ig[j], lru_lam[j], lru_w_out[j], geo)
        xa = _peer_layer(xa, m, norm_ffn[i].reshape(1, d).astype(F32), peer_w_q[i], peer_k1[i], peer_k2[i],
                         peer_u[i], peer_v[i], geo)
    out = _final_norm(xa, norm_final.reshape(1, d).astype(F32), nb * t_lat)
    return out.reshape(nb, t_lat, d)
```

```python
import functools
import math

import numpy as np
import jax
import jax.numpy as jnp
from jax import lax
from jax.experimental import pallas as pl
from jax.experimental.pallas import tpu as pltpu

F32 = jnp.float32
BF16 = jnp.bfloat16

EPS = 1e-6
GRID_W = 64
ROPE_THETA = 10000.0
ATT_HEADS = 8
ATT_KV_HEADS = 2
ATT_GROUP = ATT_HEADS // ATT_KV_HEADS
SSM_GROUP = 16
LRU_BLOCKS = 8
CONV_W = 4
CONV_LEFT = 2
LRU_C = 8.0
PEER_HEADS = 8
PEER_TOPK = 16
N_MIXERS = 3

LANES = 128
SUBLANES = 8
VMEM_LIMIT = 48 * 1024 * 1024
PEER_VMEM_LIMIT = 56 * 1024 * 1024

ROW_TILE = 512
SEQ_TILE = 256


def _cparams(sem):
    return pltpu.CompilerParams(dimension_semantics=sem, vmem_limit_bytes=VMEM_LIMIT)


def _gelu(x):
    return 0.5 * x * (1.0 + lax.erf(x * (1.0 / math.sqrt(2.0))))


def _modnorm(x, g, shift, scale):
    y = x * lax.rsqrt(jnp.mean(x * x, axis=-1, keepdims=True) + EPS) * g
    return y * (1.0 + scale) + shift


def _mod_row(m_ref, sel, k, d):
    return m_ref[pl.ds(sel, 1), k * d:(k + 1) * d]


def _row_sel(i, tiles_per_batch, n_batch):
    return jnp.minimum(i // tiles_per_batch, n_batch)


def _mod_kernel(c_ref, w_ref, b_ref, o_ref):
    c = c_ref[...]
    s = c * jax.nn.sigmoid(c)
    o_ref[0] = jnp.dot(s.astype(BF16), w_ref[0].astype(BF16), preferred_element_type=F32) + b_ref[0]


def _modulation(crows, mod_w, mod_b):
    depth, d, n = mod_w.shape
    tn = 1536
    return pl.pallas_call(
        _mod_kernel,
        out_shape=jax.ShapeDtypeStruct((depth, SUBLANES, n), F32),
        grid=(depth, n // tn),
        in_specs=[pl.BlockSpec((SUBLANES, d), lambda l, j: (0, 0)),
                  pl.BlockSpec((1, d, tn), lambda l, j: (l, 0, j)),
                  pl.BlockSpec((1, 1, tn), lambda l, j: (l, 0, j))],
        out_specs=pl.BlockSpec((1, SUBLANES, tn), lambda l, j: (l, 0, j)),
        compiler_params=_cparams(("parallel", "parallel")),
        name="modulation",
    )(crows, mod_w, mod_b.reshape(depth, 1, n))


def _norm_matmul_kernel(x_ref, m_ref, g_ref, w_ref, o_ref, *, tpb, nb, k_shift):
    d = x_ref.shape[1]
    sel = _row_sel(pl.program_id(0), tpb, nb)
    h = _modnorm(x_ref[...], g_ref[...], _mod_row(m_ref, sel, k_shift, d), _mod_row(m_ref, sel, k_shift + 1, d))
    o_ref[...] = jnp.dot(h.astype(BF16), w_ref[...], preferred_element_type=F32).astype(o_ref.dtype)


def _norm_matmul(xa, m, g, w, *, n_tiles, tpb, nb, k_shift, out_dtype=F32):
    r, d = xa.shape
    n = w.shape[1]
    return pl.pallas_call(
        functools.partial(_norm_matmul_kernel, tpb=tpb, nb=nb, k_shift=k_shift),
        out_shape=jax.ShapeDtypeStruct((r, n), out_dtype),
        grid=(n_tiles,),
        in_specs=[pl.BlockSpec((ROW_TILE, d), lambda i: (i, 0)),
                  pl.BlockSpec(m.shape, lambda i: (0, 0)),
                  pl.BlockSpec((1, d), lambda i: (0, 0)),
                  pl.BlockSpec(w.shape, lambda i: (0, 0))],
        out_specs=pl.BlockSpec((ROW_TILE, n), lambda i: (i, 0)),
        compiler_params=_cparams(("parallel",)),
        name="norm_matmul",
    )(xa, m, g, w)


def _resid_matmul_kernel(x_ref, a_ref, w_ref, m_ref, o_ref, *, tpb, nb, k_gate):
    d = x_ref.shape[1]
    sel = _row_sel(pl.program_id(0), tpb, nb)
    y = jnp.dot(a_ref[...], w_ref[...], preferred_element_type=F32)
    o_ref[...] = x_ref[...] + _mod_row(m_ref, sel, k_gate, d) * y


def _resid_matmul(xa, a, w, m, *, n_tiles, tpb, nb, k_gate):
    r, d = xa.shape
    return pl.pallas_call(
        functools.partial(_resid_matmul_kernel, tpb=tpb, nb=nb, k_gate=k_gate),
        out_shape=jax.ShapeDtypeStruct((r, d), F32),
        grid=(n_tiles,),
        in_specs=[pl.BlockSpec((ROW_TILE, d), lambda i: (i, 0)),
                  pl.BlockSpec((ROW_TILE, a.shape[1]), lambda i: (i, 0)),
                  pl.BlockSpec(w.shape, lambda i: (0, 0)),
                  pl.BlockSpec(m.shape, lambda i: (0, 0))],
        out_specs=pl.BlockSpec((ROW_TILE, d), lambda i: (i, 0)),
        input_output_aliases={0: 0},
        compiler_params=_cparams(("parallel",)),
        name="resid_matmul",
    )(xa, a, w, m)


def _attn_qkv_kernel(x_ref, m_ref, g_ref, w_ref, qg_ref, kg_ref, cos_ref, sin_ref,
                     q_ref, k_ref, v_ref, *, tpb, nb, qscale):
    d = x_ref.shape[1]
    hd = qg_ref.shape[1]
    sel = _row_sel(pl.program_id(0), tpb, nb)
    h = _modnorm(x_ref[...], g_ref[...], _mod_row(m_ref, sel, 0, d), _mod_row(m_ref, sel, 1, d))
    qkv = jnp.dot(h.astype(BF16), w_ref[...], preferred_element_type=F32)
    cosf = cos_ref[...]
    sinf = sin_ref[...]

    def norm_rope(z, gain):
        zn = z * lax.rsqrt(jnp.mean(z * z, axis=-1, keepdims=True) + EPS) * gain
        return zn * cosf + pltpu.roll(zn, hd // 2, 1) * sinf

    nq = q_ref.shape[1] // hd
    nk = k_ref.shape[1] // hd
    for j in range(nq):
        q_ref[:, j * hd:(j + 1) * hd] = (norm_rope(qkv[:, j * hd:(j + 1) * hd], qg_ref[...]) * qscale).astype(BF16)
    for j in range(nk):
        c0 = (nq + j) * hd
        k_ref[:, j * hd:(j + 1) * hd] = norm_rope(qkv[:, c0:c0 + hd], kg_ref[...]).astype(BF16)
    v_ref[...] = qkv[:, (nq + nk) * hd:].astype(BF16)


def _attn_qkv(xa, m, g, w, qg, kg, cosf, sinf, *, n_tiles, tpb, nb, qscale):
    r, d = xa.shape
    hd = qg.shape[1]
    nq, nk = ATT_HEADS * hd, ATT_KV_HEADS * hd
    tab_map = lambda i: (jnp.where(i < nb * tpb, i % tpb, tpb), 0)
    return pl.pallas_call(
        functools.partial(_attn_qkv_kernel, tpb=tpb, nb=nb, qscale=qscale),
        out_shape=(jax.ShapeDtypeStruct((r, nq), BF16), jax.ShapeDtypeStruct((r, nk), BF16),
                   jax.ShapeDtypeStruct((r, nk), BF16)),
        grid=(n_tiles,),
        in_specs=[pl.BlockSpec((ROW_TILE, d), lambda i: (i, 0)),
                  pl.BlockSpec(m.shape, lambda i: (0, 0)),
                  pl.BlockSpec((1, d), lambda i: (0, 0)),
                  pl.BlockSpec(w.shape, lambda i: (0, 0)),
                  pl.BlockSpec((1, hd), lambda i: (0, 0)),
                  pl.BlockSpec((1, hd), lambda i: (0, 0)),
                  pl.BlockSpec((ROW_TILE, hd), tab_map),
                  pl.BlockSpec((ROW_TILE, hd), tab_map)],
        out_specs=(pl.BlockSpec((ROW_TILE, nq), lambda i: (i, 0)),
                   pl.BlockSpec((ROW_TILE, nk), lambda i: (i, 0)),
                   pl.BlockSpec((ROW_TILE, nk), lambda i: (i, 0))),
        compiler_params=_cparams(("parallel",)),
        name="attn_qkv",
    )(xa, m, g, w, qg, kg, cosf, sinf)


def _flash_kernel(*refs, tq, tk, hd, n_lat_chunks):
    if n_lat_chunks:
        q_ref, kc_ref, vc_ref, kl_ref, vl_ref, o_ref, sa_ref, sb_ref = refs
    else:
        q_ref, kc_ref, vc_ref, o_ref = refs
    qs = jnp.concatenate([q_ref[:, g * hd:(g + 1) * hd] for g in range(ATT_GROUP)], axis=0)
    rows = ATT_GROUP * tq

    def scores(kblk):
        return lax.dot_general(qs, kblk, (((1,), (1,)), ((), ())), preferred_element_type=F32)

    def absorb(s, vblk, carry):
        m, l, acc = carry
        m_new = jnp.maximum(m, jnp.max(s, axis=-1, keepdims=True))
        p = jnp.exp2((s - m_new).astype(BF16))
        alpha = jnp.exp2(m - m_new)
        psum = functools.reduce(jnp.add, [p[:, i * LANES:(i + 1) * LANES] for i in range(p.shape[1] // LANES)])
        l = alpha * l + jnp.sum(psum.astype(F32), axis=-1, keepdims=True)
        acc = alpha * acc + jnp.dot(p, vblk, preferred_element_type=F32)
        return m_new, l, acc

    def lat(ref, c):
        return ref[pl.ds(pl.multiple_of(c * tk, tk), tk), :]

    carry = (jnp.full((rows, 1), -jnp.inf, F32), jnp.zeros((rows, 1), F32), jnp.zeros((rows, hd), F32))
    carry = absorb(scores(kc_ref[...]), vc_ref[...], carry)
    def lat_scores(c):
        return jnp.dot(qs, kl_ref[c], preferred_element_type=F32)

    if n_lat_chunks == 1:
        carry = absorb(lat_scores(0), lat(vl_ref, 0), carry)
    elif n_lat_chunks:
        assert n_lat_chunks % 2 == 0
        sa_ref[...] = lat_scores(0)

        def body(c2, carry):
            c = 2 * c2
            sb_ref[...] = lat_scores(c + 1)
            carry = absorb(sa_ref[...], lat(vl_ref, c), carry)
            sa_ref[...] = lat_scores(jnp.minimum(c + 2, n_lat_chunks - 1))
            return absorb(sb_ref[...], lat(vl_ref, c + 1), carry)

        carry = lax.fori_loop(0, n_lat_chunks // 2, body, carry, unroll=8 if n_lat_chunks % 16 == 0 else 1)
    _, l, acc = carry
    o = acc / l
    for g in range(ATT_GROUP):
        o_ref[:, g * hd:(g + 1) * hd] = o[g * tq:(g + 1) * tq].astype(BF16)


def _flash(q, k, v, *, nb, t_lat, n_ctx, hd, lat_queries):
    gw = ATT_GROUP * hd
    ctx_base = nb * t_lat // n_ctx
    kv_ctx = pl.BlockSpec((n_ctx, hd), lambda b, h, i: (ctx_base + b, h))
    if lat_queries:
        tq = min(256, t_lat)
        tk = min(1024, t_lat)
        nq_tiles = t_lat // tq
        q_spec = pl.BlockSpec((tq, gw), lambda b, h, i: (b * nq_tiles + i, h))
        n_chunks = t_lat // tk
        kv_lat = pl.BlockSpec((t_lat, hd), lambda b, h, i: (b, h), pipeline_mode=pl.Buffered(1))
        kt = k[:nb * t_lat].reshape(nb, n_chunks, tk, ATT_KV_HEADS, hd).transpose(0, 3, 1, 4, 2)
        kt_lat = pl.BlockSpec((None, None, n_chunks, hd, tk), lambda b, h, i: (b, h, 0, 0, 0),
                              pipeline_mode=pl.Buffered(1))
        in_specs = [q_spec, kv_ctx, kv_ctx, kt_lat, kv_lat]
        args = (q, k, v, kt, v)
        out_rows = nb * t_lat
    else:
        tq = n_ctx
        tk = n_ctx
        nq_tiles = 1
        q_spec = pl.BlockSpec((tq, gw), lambda b, h, i: (ctx_base + b, h))
        in_specs = [q_spec, kv_ctx, kv_ctx]
        args = (q, k, v)
        n_chunks = 0
        out_rows = nb * n_ctx
    return pl.pallas_call(
        functools.partial(_flash_kernel, tq=tq, tk=tk, hd=hd, n_lat_chunks=n_chunks),
        out_shape=jax.ShapeDtypeStruct((out_rows, q.shape[1]), BF16),
        grid=(nb, ATT_KV_HEADS, nq_tiles),
        in_specs=in_specs,
        out_specs=pl.BlockSpec((tq, gw), lambda b, h, i: (b * nq_tiles + i, h)),
        scratch_shapes=[pltpu.VMEM((ATT_GROUP * tq, tk), F32)] * 2 if n_chunks else [],
        compiler_params=_cparams(("parallel", "parallel", "parallel")),
        name="flash_lat" if lat_queries else "flash_ctx",
    )(*args)


def _rope_tables(t_lat, hd):
    rows = t_lat // GRID_W
    r_idx, c_idx = jnp.meshgrid(jnp.arange(rows, dtype=F32), jnp.arange(GRID_W, dtype=F32), indexing='ij')
    pairs = hd // 4
    freqs = ROPE_THETA ** (-jnp.arange(pairs, dtype=F32) / pairs)
    ang = jnp.concatenate([r_idx.reshape(-1, 1) * freqs, c_idx.reshape(-1, 1) * freqs], axis=-1)
    cos, sin = jnp.cos(ang), jnp.sin(ang)
    cosf = jnp.concatenate([cos, cos], axis=-1)
    sinf = jnp.concatenate([-sin, sin], axis=-1)
    cosf = jnp.concatenate([cosf, jnp.ones((ROW_TILE, hd), F32)], axis=0)
    sinf = jnp.concatenate([sinf, jnp.zeros((ROW_TILE, hd), F32)], axis=0)
    return cosf, sinf


def _deinterleave_perm(hd):
    return np.concatenate([np.arange(0, hd, 2), np.arange(1, hd, 2)])


def _attention_layer(xa, m, g, w_in, q_gain, k_gain, w_out, geo, ctx_out):
    nb, t_lat, n_ctx, tpb, all_tiles, n_tiles = geo
    hd = q_gain.shape[0]
    perm = _deinterleave_perm(hd)
    nqk = (ATT_HEADS + ATT_KV_HEADS) * hd
    col = np.concatenate([(np.arange(nqk) // hd) * hd + perm[np.arange(nqk) % hd],
                          np.arange(nqk, w_in.shape[1])])
    w = w_in[:, col].astype(BF16)
    cosf, sinf = _rope_tables(t_lat, hd)
    qscale = hd ** -0.5 * math.log2(math.e)
    q, k, v = _attn_qkv(xa, m, g, w, q_gain[perm].reshape(1, hd), k_gain[perm].reshape(1, hd), cosf, sinf,
                        n_tiles=all_tiles, tpb=tpb, nb=nb, qscale=qscale)
    o = _flash(q, k, v, nb=nb, t_lat=t_lat, n_ctx=n_ctx, hd=hd, lat_queries=True)
    if ctx_out:
        o_c = _flash(q, k, v, nb=nb, t_lat=t_lat, n_ctx=n_ctx, hd=hd, lat_queries=False)
        o = jnp.concatenate([o, o_c], axis=0)
    return _resid_matmul(xa, o, w_out.astype(BF16), m, n_tiles=n_tiles, tpb=tpb, nb=nb, k_gate=2)


def _seq_block(b, c, *, nb, nc_ctx, nc_lat, reverse):
    ctx_base = nb * nc_lat
    in_ctx = c < nc_ctx
    cl = c - nc_ctx
    if reverse:
        return jnp.where(in_ctx, ctx_base + b * nc_ctx + (nc_ctx - 1 - c), b * nc_lat + (nc_lat - 1 - cl))
    return jnp.where(in_ctx, ctx_base + b * nc_ctx + c, b * nc_lat + cl)


def _roll_rows(x, s, reverse):
    return pltpu.roll(x, SUBLANES - s if reverse else s, 0)


def _last_row(x, reverse):
    return jnp.broadcast_to(x[0:1, :] if reverse else x[SUBLANES - 1:SUBLANES, :], x.shape)


def _s5_scan_kernel(u_ref, bd_ref, tab_ref, cd_ref, y_ref, hs_ref, st_ref, *, tc, ns, reverse):
    c = pl.program_id(1)

    @pl.when(c == 0)
    def _():
        st_ref[...] = jnp.zeros_like(st_ref)

    wh, sh = u_ref.shape[1] // 2, ns // 2
    for g in range(2):
        ub = u_ref[:, g * wh:(g + 1) * wh].astype(BF16)
        for part in range(2):
            cols = slice(part * ns + g * sh, part * ns + (g + 1) * sh)
            hs_ref[:, cols] = jnp.dot(ub, bd_ref[0, g * wh:(g + 1) * wh, cols], preferred_element_type=F32)
    ng = tc // SUBLANES

    def body(gi, carry):
        hr, hi = carry
        r0 = pl.multiple_of(((ng - 1 - gi) if reverse else gi) * SUBLANES, SUBLANES)
        br = hs_ref[pl.ds(r0, SUBLANES), 0:ns]
        bi = hs_ref[pl.ds(r0, SUBLANES), ns:2 * ns]
        for k, s in enumerate((1, 2, 4)):
            rr, ri = _roll_rows(br, s, reverse), _roll_rows(bi, s, reverse)
            mr, mi = tab_ref[0, 2 * k], tab_ref[0, 2 * k + 1]
            br, bi = br + mr * rr - mi * ri, bi + mr * ri + mi * rr
        pr, pi_ = tab_ref[0, 6], tab_ref[0, 7]
        nr = br + pr * hr - pi_ * hi
        ni = bi + pr * hi + pi_ * hr
        hs_ref[pl.ds(r0, SUBLANES), 0:ns] = nr
        hs_ref[pl.ds(r0, SUBLANES), ns:2 * ns] = ni
        return _last_row(nr, reverse), _last_row(ni, reverse)

    hr, hi = lax.fori_loop(0, ng, body, (st_ref[:, 0:ns], st_ref[:, ns:2 * ns]))
    st_ref[:, 0:ns] = hr
    st_ref[:, ns:2 * ns] = hi
    for g in range(2):
        y_ref[:, g * wh:(g + 1) * wh] = sum(
            jnp.dot(hs_ref[:, part * ns + g * sh:part * ns + (g + 1) * sh].astype(BF16),
                    cd_ref[0, part * ns + g * sh:part * ns + (g + 1) * sh, g * wh:(g + 1) * wh],
                    preferred_element_type=F32) for part in range(2))


def _s5_scan(u, bd, tab, cd, geo, direction):
    nb, t_lat, n_ctx = geo[:3]
    r, width = u.shape
    ns = tab.shape[3]
    tc = SEQ_TILE
    nc_ctx, nc_lat = n_ctx // tc, t_lat // tc
    blk = functools.partial(_seq_block, nb=nb, nc_ctx=nc_ctx, nc_lat=nc_lat, reverse=direction == 1)
    return pl.pallas_call(
        functools.partial(_s5_scan_kernel, tc=tc, ns=ns, reverse=direction == 1),
        out_shape=jax.ShapeDtypeStruct((r, width), F32),
        grid=(nb, nc_ctx + nc_lat),
        in_specs=[pl.BlockSpec((tc, width), lambda b, c: (blk(b, c), 0)),
                  pl.BlockSpec((1, width, 2 * ns), lambda b, c: (direction, 0, 0)),
                  pl.BlockSpec((1, 8, SUBLANES, ns), lambda b, c: (direction, 0, 0, 0)),
                  pl.BlockSpec((1, 2 * ns, width), lambda b, c: (direction, 0, 0))],
        out_specs=pl.BlockSpec((tc, width), lambda b, c: (blk(b, c), 0)),
        scratch_shapes=[pltpu.VMEM((tc, 2 * ns), F32), pltpu.VMEM((SUBLANES, 2 * ns), F32)],
        compiler_params=_cparams(("arbitrary", "arbitrary")),
        name="s5_scan_bwd" if direction else "s5_scan_fwd",
    )(u, bd, tab, cd)


def _s5_out_kernel(x_ref, u_ref, yf_ref, yb_ref, dsk_ref, wg_ref, wo_ref, m_ref, o_ref, *, tpb, nb):
    d = x_ref.shape[1]
    sel = _row_sel(pl.program_id(0), tpb, nb)
    y = dsk_ref[...] * u_ref[...] + yf_ref[...] + yb_ref[...]
    z = _gelu(y)
    z = z * jax.nn.sigmoid(jnp.dot(z.astype(BF16), wg_ref[...], preferred_element_type=F32))
    out = jnp.dot(z.astype(BF16), wo_ref[...], preferred_element_type=F32)
    o_ref[...] = x_ref[...] + _mod_row(m_ref, sel, 2, d) * out


def _s5_out(xa, u, yf, yb, dsk, wg, wo, m, *, n_tiles, tpb, nb):
    r, d = xa.shape
    width = u.shape[1]
    return pl.pallas_call(
        functools.partial(_s5_out_kernel, tpb=tpb, nb=nb),
        out_shape=jax.ShapeDtypeStruct((r, d), F32),
        grid=(n_tiles,),
        in_specs=[pl.BlockSpec((ROW_TILE, d), lambda i: (i, 0)),
                  pl.BlockSpec((ROW_TILE, width), lambda i: (i, 0)),
                  pl.BlockSpec((ROW_TILE, width), lambda i: (i, 0)),
                  pl.BlockSpec((ROW_TILE, width), lambda i: (i, 0)),
                  pl.BlockSpec((1, width), lambda i: (0, 0)),
                  pl.BlockSpec(wg.shape, lambda i: (0, 0)),
                  pl.BlockSpec(wo.shape, lambda i: (0, 0)),
                  pl.BlockSpec(m.shape, lambda i: (0, 0))],
        out_specs=pl.BlockSpec((ROW_TILE, d), lambda i: (i, 0)),
        input_output_aliases={0: 0},
        compiler_params=_cparams(("parallel",)),
        name="s5_out",
    )(xa, u, yf, yb, dsk, wg, wo, m)


def _s5_layer(xa, m, g, w_in, lam_re, lam_im, log_dt, b_re, b_im, c_re, c_im, d_skip, w_glu, w_out, geo):
    nb, t_lat, n_ctx, tpb, all_tiles, n_tiles = geo
    groups, p_state = lam_re.shape[1], lam_re.shape[2]
    width = w_in.shape[1]
    ns = groups * p_state
    lam = lax.complex(lam_re.astype(F32), lam_im.astype(F32))
    dt = jnp.exp(log_dt.astype(F32))[..., None]
    lam_bar = jnp.exp(lam * dt)
    b_bar = ((lam_bar - 1.0) / lam)[..., None] * lax.complex(b_re.astype(F32), b_im.astype(F32))
    eye = jnp.eye(groups, dtype=F32)

    def block_diag_in(z):
        return jnp.einsum('dgpc,gh->dgchp', z, eye).reshape(2, groups * SSM_GROUP, ns)

    def block_diag_out(z):
        return jnp.einsum('dgcp,gh->dgphc', z, eye).reshape(2, ns, groups * SSM_GROUP)

    bd = jnp.concatenate([block_diag_in(jnp.real(b_bar)), block_diag_in(jnp.imag(b_bar))], axis=-1).astype(BF16)
    cd = jnp.concatenate([block_diag_out(c_re.astype(F32)), block_diag_out(-c_im.astype(F32))], axis=1).astype(BF16)
    pw = [lam_bar.reshape(2, ns)]
    for _ in range(SUBLANES - 1):
        pw.append(pw[-1] * pw[0])
    row = jnp.arange(SUBLANES)
    tabs = []
    for dirn in range(2):
        rev = dirn == 1
        planes = []
        for s in (1, 2, 4):
            keep = (row + s <= SUBLANES - 1) if rev else (row >= s)
            ms = jnp.where(keep[:, None], pw[s - 1][dirn][None, :], 0.0)
            planes += [jnp.real(ms), jnp.imag(ms)]
        pows = jnp.stack([pw[(SUBLANES - 1 - t) if rev else t][dirn] for t in range(SUBLANES)], axis=0)
        planes += [jnp.real(pows), jnp.imag(pows)]
        tabs.append(jnp.stack(planes, axis=0))
    tab = jnp.stack(tabs, axis=0).astype(F32)

    u = _norm_matmul(xa, m, g, w_in.astype(BF16), n_tiles=all_tiles, tpb=tpb, nb=nb, k_shift=0)
    yf = _s5_scan(u, bd, tab, cd, geo, 0)
    yb = _s5_scan(u, bd, tab, cd, geo, 1)
    return _s5_out(xa, u, yf, yb, d_skip.reshape(1, width).astype(F32), w_glu.astype(BF16), w_out.astype(BF16), m,
                   n_tiles=n_tiles, tpb=tpb, nb=nb)


def _lru_coef_kernel(cur_ref, prev_ref, next_ref, cw_ref, cb_ref, wrg_ref, brg_ref, wig_ref, big_ref, cn_ref,
                     a_ref, b_ref, buf_ref, *, tm, tps_lat, tps_ctx, n_lat_tiles):
    i = pl.program_id(0)
    j = i - n_lat_tiles
    first = jnp.where(i < n_lat_tiles, i % tps_lat == 0, j % tps_ctx == 0)
    last = jnp.where(i < n_lat_tiles, i % tps_lat == tps_lat - 1, j % tps_ctx == tps_ctx - 1)
    h = SUBLANES
    buf_ref[0:h, :] = jnp.where(first, 0.0, prev_ref[...])
    buf_ref[h:h + tm, :] = cur_ref[...]
    buf_ref[h + tm:h + tm + h, :] = jnp.where(last, 0.0, next_ref[...])
    xr = cb_ref[...] + sum(cw_ref[k:k + 1, :] * buf_ref[h - CONV_LEFT + k:h - CONV_LEFT + k + tm, :]
                           for k in range(CONV_W))
    nblk = wrg_ref.shape[1]
    bw = wrg_ref.shape[2]
    for n in range(nblk):
        sl = slice(n * bw, (n + 1) * bw)
        xs = xr[:, sl]
        xb = xs.astype(BF16)
        for d in range(2):
            r = jax.nn.sigmoid(jnp.dot(xb, wrg_ref[d, n], preferred_element_type=F32) + brg_ref[d, :, sl])
            ig = jax.nn.sigmoid(jnp.dot(xb, wig_ref[d, n], preferred_element_type=F32) + big_ref[d, :, sl])
            log_a = cn_ref[d, :, sl] * r
            a_ref[d, :, sl] = jnp.exp(log_a)
            b_ref[d, :, sl] = jnp.sqrt(1.0 - jnp.exp(2.0 * log_a)) * (ig * xs)


def _lru_coef(gx, conv_w, conv_b, wrg, brg, wig, big, cneg, geo):
    nb, t_lat, n_ctx = geo[:3]
    r = gx.shape[0]
    width = conv_w.shape[1]
    tm = SEQ_TILE
    hb = tm // SUBLANES
    n_lat_tiles = nb * t_lat // tm
    last_halo = r // SUBLANES - 1
    vec = lambda a: pl.BlockSpec(a.shape, lambda i: (0,) * a.ndim)
    return pl.pallas_call(
        functools.partial(_lru_coef_kernel, tm=tm, tps_lat=t_lat // tm, tps_ctx=n_ctx // tm, n_lat_tiles=n_lat_tiles),
        out_shape=(jax.ShapeDtypeStruct((2, r, width), F32), jax.ShapeDtypeStruct((2, r, width), F32)),
        grid=(r // tm,),
        in_specs=[pl.BlockSpec((tm, width), lambda i: (i, 1)),
                  pl.BlockSpec((SUBLANES, width), lambda i: (jnp.maximum(i * hb - 1, 0), 1)),
                  pl.BlockSpec((SUBLANES, width), lambda i: (jnp.minimum((i + 1) * hb, last_halo), 1)),
                  vec(conv_w), vec(conv_b), vec(wrg), vec(brg), vec(wig), vec(big), vec(cneg)],
        out_specs=(pl.BlockSpec((2, tm, width), lambda i: (0, i, 0)),
                   pl.BlockSpec((2, tm, width), lambda i: (0, i, 0))),
        scratch_shapes=[pltpu.VMEM((tm + 2 * SUBLANES, width), F32)],
        compiler_params=_cparams(("parallel",)),
        name="lru_coef",
    )(gx, gx, gx, conv_w, conv_b, wrg, brg, wig, big, cneg)


def _lru_scan_kernel(a_ref, b_ref, o_ref, st_ref, *, tc, reverse):
    c = pl.program_id(1)

    @pl.when(c == 0)
    def _():
        st_ref[...] = jnp.zeros_like(st_ref)

    width = a_ref.shape[-1]
    row = lax.broadcasted_iota(jnp.int32, (SUBLANES, width), 0)
    ng = tc // SUBLANES

    def body(gi, h):
        r0 = pl.multiple_of(((ng - 1 - gi) if reverse else gi) * SUBLANES, SUBLANES)
        a = a_ref[0, pl.ds(r0, SUBLANES), :]
        b = b_ref[0, pl.ds(r0, SUBLANES), :]
        for s in (1, 2, 4):
            keep = (row < SUBLANES - s) if reverse else (row >= s)
            ar = jnp.where(keep, _roll_rows(a, s, reverse), 1.0)
            br = jnp.where(keep, _roll_rows(b, s, reverse), 0.0)
            b = a * br + b
            a = a * ar
        hn = a * h + b
        o_ref[pl.ds(r0, SUBLANES), :] = hn
        return _last_row(hn, reverse)

    st_ref[...] = lax.fori_loop(0, ng, body, st_ref[...])


def _lru_scan(a, b, geo, direction):
    nb, t_lat, n_ctx = geo[:3]
    _, r, width = a.shape
    tc = SEQ_TILE
    nc_ctx, nc_lat = n_ctx // tc, t_lat // tc
    blk = functools.partial(_seq_block, nb=nb, nc_ctx=nc_ctx, nc_lat=nc_lat, reverse=direction == 1)
    spec = pl.BlockSpec((1, tc, width), lambda b_, c: (direction, blk(b_, c), 0))
    return pl.pallas_call(
        functools.partial(_lru_scan_kernel, tc=tc, reverse=direction == 1),
        out_shape=jax.ShapeDtypeStruct((r, width), F32),
        grid=(nb, nc_ctx + nc_lat),
        in_specs=[spec, spec],
        out_specs=pl.BlockSpec((tc, width), lambda b_, c: (blk(b_, c), 0)),
        scratch_shapes=[pltpu.VMEM((SUBLANES, width), F32)],
        compiler_params=_cparams(("arbitrary", "arbitrary")),
        name="lru_scan_bwd" if direction else "lru_scan_fwd",
    )(a, b)


def _lru_out_kernel(x_ref, gate_ref, sf_ref, sb_ref, wo_ref, m_ref, o_ref, *, tpb, nb):
    d = x_ref.shape[1]
    sel = _row_sel(pl.program_id(0), tpb, nb)
    y = (sf_ref[...] + sb_ref[...]) * _gelu(gate_ref[...])
    out = jnp.dot(y.astype(BF16), wo_ref[...], preferred_element_type=F32)
    o_ref[...] = x_ref[...] + _mod_row(m_ref, sel, 2, d) * out


def _lru_out(xa, gx, sf, sb, wo, m, *, n_tiles, tpb, nb):
    r, d = xa.shape
    width = sf.shape[1]
    return pl.pallas_call(
        functools.partial(_lru_out_kernel, tpb=tpb, nb=nb),
        out_shape=jax.ShapeDtypeStruct((r, d), F32),
        grid=(n_tiles,),
        in_specs=[pl.BlockSpec((ROW_TILE, d), lambda i: (i, 0)),
                  pl.BlockSpec((ROW_TILE, width), lambda i: (i, 0)),
                  pl.BlockSpec((ROW_TILE, width), lambda i: (i, 0)),
                  pl.BlockSpec((ROW_TILE, width), lambda i: (i, 0)),
                  pl.BlockSpec(wo.shape, lambda i: (0, 0)),
                  pl.BlockSpec(m.shape, lambda i: (0, 0))],
        out_specs=pl.BlockSpec((ROW_TILE, d), lambda i: (i, 0)),
        input_output_aliases={0: 0},
        compiler_params=_cparams(("parallel",)),
        name="lru_out",
    )(xa, gx, sf, sb, wo, m)


def _lru_layer(xa, m, g, w_in, conv_w, conv_b, w_rg, b_rg, w_ig, b_ig, lam, w_out, geo):
    nb, t_lat, n_ctx, tpb, all_tiles, n_tiles = geo
    width = conv_w.shape[1]
    gx = _norm_matmul(xa, m, g, w_in.astype(BF16), n_tiles=all_tiles, tpb=tpb, nb=nb, k_shift=0)
    cneg = (-LRU_C * jax.nn.softplus(-lam.astype(F32))).reshape(2, 1, width)
    a, b = _lru_coef(gx, conv_w.astype(F32), conv_b.reshape(1, width).astype(F32),
                     w_rg.astype(BF16), b_rg.reshape(2, 1, width).astype(F32),
                     w_ig.astype(BF16), b_ig.reshape(2, 1, width).astype(F32), cneg, geo)
    sf = _lru_scan(a, b, geo, 0)
    sb = _lru_scan(a, b, geo, 1)
    return _lru_out(xa, gx, sf, sb, w_out.astype(BF16), m, n_tiles=n_tiles, tpb=tpb, nb=nb)


def _oddeven_merge_sort_pairs(n):
    pairs = []
    p = 1
    while p < n:
        k = p
        while k >= 1:
            for j in range(k % p, n - k, 2 * k):
                for i in range(min(k, n - j - k)):
                    if (i + j) // (2 * p) == (i + j + k) // (2 * p):
                        pairs.append((i + j, i + j + k))
            k //= 2
        p *= 2
    return pairs


_SORT16 = _oddeven_merge_sort_pairs(PEER_TOPK)


def _ce(v, i, j):
    a, b = v[i], v[j]
    v[i] = jnp.maximum(a, b)
    v[j] = jnp.minimum(a, b)


def _bitonic_merge_desc(v):
    n = len(v)
    k = n // 2
    while k >= 1:
        for i in range(n):
            if i & k == 0:
                _ce(v, i, i | k)
        k //= 2


def _top16_desc(blk):
    n = PEER_TOPK
    v = [blk[SUBLANES * a:SUBLANES * (a + 1), :] for a in range(n)]
    for (i, j) in _SORT16:
        _ce(v, i, j)
    for shift in (4, 2, 1):
        o = [pltpu.roll(x, shift, 0) for x in v]
        v = [jnp.maximum(v[i], o[n - 1 - i]) for i in range(n)]
        _bitonic_merge_desc(v)
    return v


_KEY_FLOOR = 2.0 ** -100


def _peer_route(s, nk):
    nh, k = PEER_HEADS, PEER_TOPK
    w = s.shape[1]
    low = nk - 1
    key = lax.broadcasted_iota(jnp.int32, (nk, w), 0)
    sub = lax.broadcasted_iota(jnp.int32, (SUBLANES, w), 0)
    vals, idxs = [], []
    for half in range(2):
        pk = [jnp.zeros((SUBLANES, w), F32) for _ in range(k)]
        for hh in range(nh):
            blk = s[(half * nh + hh) * nk:(half * nh + hh + 1) * nk, :]
            bits = lax.bitcast_convert_type(jnp.where(jnp.abs(blk) < _KEY_FLOOR, _KEY_FLOOR, blk), jnp.int32)
            neg = (bits >> 31) & low
            keys = lax.bitcast_convert_type((bits & ~low) | ((low - key) ^ neg), F32)
            top = _top16_desc(keys)
            pk = [jnp.where(sub == hh, top[i], pk[i]) for i in range(k)]
        pkb = [lax.bitcast_convert_type(x, jnp.int32) for x in pk]
        idxs.append([low - ((x & low) ^ ((x >> 31) & low)) for x in pkb])
        vals.append([lax.bitcast_convert_type(x & ~low, F32) for x in pkb])
    cbits = k * k - 1

    def pack(v, code):
        b = lax.bitcast_convert_type(jnp.where(jnp.abs(v) < _KEY_FLOOR, _KEY_FLOOR, v), jnp.int32)
        return lax.bitcast_convert_type((b & ~cbits) | ((cbits - code) ^ ((b >> 31) & cbits)), F32)

    top = [pack(vals[0][0] + vals[1][j], j) for j in range(k)]
    for i in range(1, k):
        n = k // (i + 1)
        for t in range(n):
            top[k - 1 - t] = jnp.maximum(top[k - 1 - t], pack(vals[0][i] + vals[1][t], i * k + t))
        _bitonic_merge_desc(top)
    tb = [lax.bitcast_convert_type(x, jnp.int32) for x in top]
    codes = [cbits - ((x & cbits) ^ ((x >> 31) & cbits)) for x in tb]
    best = [lax.bitcast_convert_type(x & ~cbits, F32) for x in tb]
    ex = [jnp.exp(b - best[0]) for b in best]
    zinv = 1.0 / functools.reduce(lambda a, b: a + b, ex)
    i1, i2 = [], []
    for hc in codes:
        ri, rj = hc >> (k.bit_length() - 1), hc & (k - 1)
        i1.append(functools.reduce(lambda a, b: a + b, [jnp.where(ri == i, idxs[0][i], 0) for i in range(k)]))
        i2.append(functools.reduce(lambda a, b: a + b, [jnp.where(rj == j, idxs[1][j], 0) for j in range(k)]))
    cat = lambda xs: jnp.concatenate(xs, axis=0)
    return cat(i1), cat(i2), cat([e * zinv for e in ex])


def _peer_score_kernel(x_ref, m_ref, g_ref, wq_ref, kk_ref, h_ref, i1_ref, i2_ref, gate_ref, s_ref, *, tpb, nb):
    d = x_ref.shape[1]
    nk = kk_ref.shape[1]
    kd = kk_ref.shape[2]
    sel = _row_sel(pl.program_id(0), tpb, nb)
    h = _modnorm(x_ref[...], g_ref[...], _mod_row(m_ref, sel, 3, d), _mod_row(m_ref, sel, 4, d)).astype(BF16)
    h_ref[...] = h
    q = jnp.dot(h, wq_ref[...], preferred_element_type=F32).astype(BF16)
    for half in range(2):
        for hh in range(PEER_HEADS):
            jq = hh * 2 + half
            blk = half * PEER_HEADS + hh
            s_ref[blk * nk:(blk + 1) * nk, :] = lax.dot_general(
                kk_ref[blk], q[:, jq * kd:(jq + 1) * kd], (((1,), (1,)), ((), ())),
                preferred_element_type=F32)
    i1, i2, gate = _peer_route(s_ref, nk)
    i1_ref[...] = i1.astype(F32).T
    i2_ref[...] = i2.astype(F32).T
    gate_ref[...] = gate.T


def _peer_scores(xa, m, g, wq, kk, *, n_tiles, tpb, nb):
    r, d = xa.shape
    nk = kk.shape[1]
    slots = PEER_HEADS * PEER_TOPK
    slot_spec = pl.BlockSpec((ROW_TILE, slots), lambda i: (i, 0))
    slot_shape = jax.ShapeDtypeStruct((r, slots), F32)
    return pl.pallas_call(
        functools.partial(_peer_score_kernel, tpb=tpb, nb=nb),
        out_shape=(jax.ShapeDtypeStruct((r, d), BF16), slot_shape, slot_shape, slot_shape),
        grid=(n_tiles,),
        in_specs=[pl.BlockSpec((ROW_TILE, d), lambda i: (i, 0)),
                  pl.BlockSpec(m.shape, lambda i: (0, 0)),
                  pl.BlockSpec((1, d), lambda i: (0, 0)),
                  pl.BlockSpec(wq.shape, lambda i: (0, 0)),
                  pl.BlockSpec(kk.shape, lambda i: (0, 0, 0))],
        out_specs=(pl.BlockSpec((ROW_TILE, d), lambda i: (i, 0)), slot_spec, slot_spec, slot_spec),
        scratch_shapes=[pltpu.VMEM((2 * PEER_HEADS * nk, ROW_TILE), F32)],
        compiler_params=_cparams(("parallel",)),
        name="peer_scores",
    )(xa, m, g, wq, kk)


PEER_EXPERT_BLOCK = 2048


def _peer_dense_kernel(x_ref, h_ref, i1_ref, i2_ref, gate_ref, ut_ref, v_ref, m_ref, o_ref,
                       acc_ref, w_ref, *, tpb, nb, nk, pitch):
    d = x_ref.shape[1]
    tm = x_ref.shape[0]
    eb = ut_ref.shape[1]
    n_i1 = eb // nk
    rows = nk // 2
    steps = rows // n_i1
    j = pl.program_id(1)
    hi_mask = jnp.int32(-65536)

    @pl.when(j == 0)
    def _():
        acc_ref[...] = jnp.zeros_like(acc_ref)
        slots = i1_ref.shape[1]
        key = lax.broadcasted_iota(jnp.int32, (nk, slots), 0).astype(F32)

        def chunk(ci, carry):
            r0 = pl.multiple_of(ci * SUBLANES, SUBLANES)
            i1c = i1_ref[pl.ds(r0, SUBLANES), :]
            i2c = i2_ref[pl.ds(r0, SUBLANES), :]
            gc = gate_ref[pl.ds(r0, SUBLANES), :]
            for r in range(SUBLANES):
                lhs = jnp.where(key == i1c[r:r + 1, :], gc[r:r + 1, :], 0.0).astype(BF16)
                rhs = jnp.where(key == i2c[r:r + 1, :], 1.0, 0.0).astype(BF16)
                wt = lax.dot_general(lhs, rhs, (((1,), (1,)), ((), ())), preferred_element_type=F32)
                bits = lax.bitcast_convert_type(wt, jnp.int32) + 0x8000
                word = (bits[rows:] & hi_mask) | lax.shift_right_logical(bits[:rows], 16)
                w_ref[pl.ds(pl.multiple_of((r0 + r) * pitch, SUBLANES), rows), :] = word
            return carry

        lax.fori_loop(0, tm // SUBLANES, chunk, 0, unroll=8)

    s = jnp.dot(h_ref[...], ut_ref[...], preferred_element_type=F32)
    jj = j % steps
    shift = jnp.where(j < steps, 16, 0)
    p = []
    for ii in range(n_i1):
        word = w_ref[pl.ds(jj * n_i1 + ii, tm, stride=pitch), :]
        wb = lax.bitcast_convert_type(lax.shift_left(word, jnp.broadcast_to(shift, word.shape)) & hi_mask, F32)
        p.append((wb * _gelu(s[:, ii * nk:(ii + 1) * nk])).astype(BF16))
    acc_ref[...] += jnp.dot(jnp.concatenate(p, axis=1), v_ref[...], preferred_element_type=F32)

    @pl.when(j == pl.num_programs(1) - 1)
    def _():
        sel = _row_sel(pl.program_id(0), tpb, nb)
        o_ref[...] = x_ref[...] + _mod_row(m_ref, sel, 5, d) * acc_ref[...]


def _peer_dense(xa, h, i1, i2, gate, ut, v, m, *, n_tiles, tpb, nb, nk):
    r, d = xa.shape
    n_exp = v.shape[0]
    eb = PEER_EXPERT_BLOCK
    tm = ROW_TILE
    slots = i1.shape[1]
    rows = nk // 2
    assert rows % (eb // nk) == 0 and n_exp == nk * nk
    pitch = rows + SUBLANES if (rows // SUBLANES) % 2 == 0 else rows
    slot_spec = pl.BlockSpec((tm, slots), lambda i, j: (i, 0))
    return pl.pallas_call(
        functools.partial(_peer_dense_kernel, tpb=tpb, nb=nb, nk=nk, pitch=pitch),
        out_shape=jax.ShapeDtypeStruct((r, d), F32),
        grid=(n_tiles, n_exp // eb),
        in_specs=[pl.BlockSpec((tm, d), lambda i, j: (i, 0)),
                  pl.BlockSpec((tm, d), lambda i, j: (i, 0)),
                  slot_spec, slot_spec, slot_spec,
                  pl.BlockSpec((d, eb), lambda i, j: (0, j)),
                  pl.BlockSpec((eb, d), lambda i, j: (j, 0)),
                  pl.BlockSpec(m.shape, lambda i, j: (0, 0))],
        out_specs=pl.BlockSpec((tm, d), lambda i, j: (i, 0)),
        scratch_shapes=[pltpu.VMEM((tm, d), F32),
                        pltpu.VMEM((tm * pitch, nk), jnp.int32)],
        input_output_aliases={0: 0},
        compiler_params=pltpu.CompilerParams(dimension_semantics=("parallel", "arbitrary"),
                                             vmem_limit_bytes=PEER_VMEM_LIMIT),
        name="peer_dense",
    )(xa, h, i1, i2, gate, ut, v, m)


def _peer_layer(xa, m, g, w_q, k1, k2, u_tab, v_tab, geo):
    nb, t_lat, n_ctx, tpb, all_tiles, n_tiles = geo
    nk = k1.shape[1]
    kk = jnp.concatenate([k1, k2], axis=0).astype(BF16)
    h, i1, i2, gate = _peer_scores(xa, m, g, w_q.astype(BF16), kk, n_tiles=n_tiles, tpb=tpb, nb=nb)
    return _peer_dense(xa, h, i1, i2, gate, u_tab.astype(BF16).T, v_tab.astype(BF16), m,
                       n_tiles=n_tiles, tpb=tpb, nb=nb, nk=nk)


def _final_norm_kernel(x_ref, g_ref, o_ref):
    x = x_ref[...]
    o_ref[...] = x * lax.rsqrt(jnp.mean(x * x, axis=-1, keepdims=True) + EPS) * g_ref[...]


def _final_norm(xa, g, n_rows):
    d = xa.shape[1]
    return pl.pallas_call(
        _final_norm_kernel,
        out_shape=jax.ShapeDtypeStruct((n_rows, d), F32),
        grid=(n_rows // ROW_TILE,),
        in_specs=[pl.BlockSpec((ROW_TILE, d), lambda i: (i, 0)), pl.BlockSpec((1, d), lambda i: (0, 0))],
        out_specs=pl.BlockSpec((ROW_TILE, d), lambda i: (i, 0)),
        compiler_params=_cparams(("parallel",)),
        name="final_norm",
    )(xa, g)


def kernel(x, c, ctx, c_ctx, mod_w, mod_b, norm_mix, norm_ffn, norm_final, attn_w_in, attn_q_gain, attn_k_gain, attn_w_out, ssm_w_in, ssm_lam_re, ssm_lam_im, ssm_log_dt, ssm_b_re, ssm_b_im, ssm_c_re, ssm_c_im, ssm_d, ssm_w_glu, ssm_w_out, lru_w_in, lru_conv_w, lru_conv_b, lru_w_rg, lru_b_rg, lru_w_ig, lru_b_ig, lru_lam, lru_w_out, peer_w_q, peer_k1, peer_k2, peer_u, peer_v):
    nb, t_lat, d = x.shape
    n_ctx = ctx.shape[1]
    depth = mod_w.shape[0]
    assert t_lat % ROW_TILE == 0 and (nb * n_ctx) % ROW_TILE == 0 and n_ctx % SEQ_TILE == 0
    assert nb + 1 <= SUBLANES and t_lat % GRID_W == 0
    tpb = t_lat // ROW_TILE
    lat_tiles = nb * tpb
    all_tiles = lat_tiles + nb * n_ctx // ROW_TILE

    xa = jnp.concatenate([x.reshape(nb * t_lat, d), ctx.reshape(nb * n_ctx, d)], axis=0).astype(F32)
    crows = jnp.concatenate([c.astype(F32), c_ctx.reshape(1, d).astype(F32),
                             jnp.zeros((SUBLANES - nb - 1, d), F32)], axis=0)
    mods = _modulation(crows, mod_w.astype(F32), mod_b.astype(F32))

    for i in range(depth):
        ctx_out = i < depth - 1
        geo = (nb, t_lat, n_ctx, tpb, all_tiles, all_tiles if ctx_out else lat_tiles)
        kind, j = i % N_MIXERS, i // N_MIXERS
        m = mods[i]
        g_mix = norm_mix[i].reshape(1, d).astype(F32)
        if kind == 0:
            xa = _attention_layer(xa, m, g_mix, attn_w_in[j], attn_q_gain[j], attn_k_gain[j], attn_w_out[j],
                                  geo, ctx_out)
        elif kind == 1:
            xa = _s5_layer(xa, m, g_mix, ssm_w_in[j], ssm_lam_re[j], ssm_lam_im[j], ssm_log_dt[j],
                           ssm_b_re[j], ssm_b_im[j], ssm_c_re[j], ssm_c_im[j], ssm_d[j],
                           ssm_w_glu[j], ssm_w_out[j], geo)
        else:
            xa = _lru_layer(xa, m, g_mix, lru_w_in[j], lru_conv_w[j], lru_conv_b[j], lru_w_rg[j], lru_b_rg[j],
                            lru_w_ig[j], lru_b_ig[j], lru_lam[j], lru_w_out[j], geo)
        xa = _peer_layer(xa, m, norm_ffn[i].reshape(1, d).astype(F32), peer_w_q[i], peer_k1[i], peer_k2[i],
                         peer_u[i], peer_v[i], geo)
    out = _final_norm(xa, norm_final.reshape(1, d).astype(F32), nb * t_lat)
    return out.reshape(nb, t_lat, d)
```

```python
import functools
import math

import numpy as np
import jax
import jax.numpy as jnp
from jax import lax
from jax.experimental import pallas as pl
from jax.experimental.pallas import tpu as pltpu

F32 = jnp.float32
BF16 = jnp.bfloat16

EPS = 1e-6
GRID_W = 64
ROPE_THETA = 10000.0
ATT_HEADS = 8
ATT_KV_HEADS = 2
ATT_GROUP = ATT_HEADS // ATT_KV_HEADS
SSM_GROUP = 16
LRU_BLOCKS = 8
CONV_W = 4
CONV_LEFT = 2
LRU_C = 8.0
PEER_HEADS = 8
PEER_TOPK = 16
N_MIXERS = 3

LANES = 128
SUBLANES = 8
VMEM_LIMIT = 48 * 1024 * 1024
PEER_VMEM_LIMIT = 56 * 1024 * 1024

ROW_TILE = 512
SEQ_TILE = 256


def _cparams(sem):
    return pltpu.CompilerParams(dimension_semantics=sem, vmem_limit_bytes=VMEM_LIMIT)


def _gelu(x):
    return 0.5 * x * (1.0 + lax.erf(x * (1.0 / math.sqrt(2.0))))


def _sigmoid(x):
    return 0.5 * jnp.tanh(0.5 * x) + 0.5


def _modnorm(x, g, shift, scale):
    y = x * lax.rsqrt(jnp.mean(x * x, axis=-1, keepdims=True) + EPS) * g
    return y * (1.0 + scale) + shift


def _mod_row(m_ref, sel, k, d):
    return m_ref[pl.ds(sel, 1), k * d:(k + 1) * d]


def _row_sel(i, tiles_per_batch, n_batch):
    return jnp.minimum(i // tiles_per_batch, n_batch)


def _mod_kernel(c_ref, w_ref, b_ref, o_ref):
    c = c_ref[...]
    s = c * jax.nn.sigmoid(c)
    o_ref[0] = jnp.dot(s.astype(BF16), w_ref[0].astype(BF16), preferred_element_type=F32) + b_ref[0]


def _modulation(crows, mod_w, mod_b):
    depth, d, n = mod_w.shape
    tn = 1536
    return pl.pallas_call(
        _mod_kernel,
        out_shape=jax.ShapeDtypeStruct((depth, SUBLANES, n), F32),
        grid=(depth, n // tn),
        in_specs=[pl.BlockSpec((SUBLANES, d), lambda l, j: (0, 0)),
                  pl.BlockSpec((1, d, tn), lambda l, j: (l, 0, j)),
                  pl.BlockSpec((1, 1, tn), lambda l, j: (l, 0, j))],
        out_specs=pl.BlockSpec((1, SUBLANES, tn), lambda l, j: (l, 0, j)),
        compiler_params=_cparams(("parallel", "parallel")),
        name="modulation",
    )(crows, mod_w, mod_b.reshape(depth, 1, n))


def _norm_matmul_kernel(x_ref, m_ref, g_ref, w_ref, o_ref, *, tpb, nb, k_shift):
    d = x_ref.shape[1]
    sel = _row_sel(pl.program_id(0), tpb, nb)
    h = _modnorm(x_ref[...], g_ref[...], _mod_row(m_ref, sel, k_shift, d), _mod_row(m_ref, sel, k_shift + 1, d))
    o_ref[...] = jnp.dot(h.astype(BF16), w_ref[...], preferred_element_type=F32).astype(o_ref.dtype)


def _norm_matmul(xa, m, g, w, *, n_tiles, tpb, nb, k_shift, out_dtype=F32):
    r, d = xa.shape
    n = w.shape[1]
    return pl.pallas_call(
        functools.partial(_norm_matmul_kernel, tpb=tpb, nb=nb, k_shift=k_shift),
        out_shape=jax.ShapeDtypeStruct((r, n), out_dtype),
        grid=(n_tiles,),
        in_specs=[pl.BlockSpec((ROW_TILE, d), lambda i: (i, 0)),
                  pl.BlockSpec(m.shape, lambda i: (0, 0)),
                  pl.BlockSpec((1, d), lambda i: (0, 0)),
                  pl.BlockSpec(w.shape, lambda i: (0, 0))],
        out_specs=pl.BlockSpec((ROW_TILE, n), lambda i: (i, 0)),
        compiler_params=_cparams(("parallel",)),
        name="norm_matmul",
    )(xa, m, g, w)


def _resid_matmul_kernel(x_ref, a_ref, w_ref, m_ref, o_ref, *, tpb, nb, k_gate):
    d = x_ref.shape[1]
    sel = _row_sel(pl.program_id(0), tpb, nb)
    y = jnp.dot(a_ref[...], w_ref[...], preferred_element_type=F32)
    o_ref[...] = x_ref[...] + _mod_row(m_ref, sel, k_gate, d) * y


def _resid_matmul(xa, a, w, m, *, n_tiles, tpb, nb, k_gate):
    r, d = xa.shape
    return pl.pallas_call(
        functools.partial(_resid_matmul_kernel, tpb=tpb, nb=nb, k_gate=k_gate),
        out_shape=jax.ShapeDtypeStruct((r, d), F32),
        grid=(n_tiles,),
        in_specs=[pl.BlockSpec((ROW_TILE, d), lambda i: (i, 0)),
                  pl.BlockSpec((ROW_TILE, a.shape[1]), lambda i: (i, 0)),
                  pl.BlockSpec(w.shape, lambda i: (0, 0)),
                  pl.BlockSpec(m.shape, lambda i: (0, 0))],
        out_specs=pl.BlockSpec((ROW_TILE, d), lambda i: (i, 0)),
        input_output_aliases={0: 0},
        compiler_params=_cparams(("parallel",)),
        name="resid_matmul",
    )(xa, a, w, m)


def _attn_qkv_kernel(x_ref, m_ref, g_ref, w_ref, qg_ref, kg_ref, cos_ref, sin_ref,
                     q_ref, k_ref, v_ref, *, tpb, nb, qscale):
    d = x_ref.shape[1]
    hd = qg_ref.shape[1]
    sel = _row_sel(pl.program_id(0), tpb, nb)
    h = _modnorm(x_ref[...], g_ref[...], _mod_row(m_ref, sel, 0, d), _mod_row(m_ref, sel, 1, d))
    qkv = jnp.dot(h.astype(BF16), w_ref[...], preferred_element_type=F32)
    cosf = cos_ref[...]
    sinf = sin_ref[...]

    def norm_rope(z, gain):
        zn = z * lax.rsqrt(jnp.mean(z * z, axis=-1, keepdims=True) + EPS) * gain
        return zn * cosf + pltpu.roll(zn, hd // 2, 1) * sinf

    nq = q_ref.shape[1] // hd
    nk = k_ref.shape[1] // hd
    for j in range(nq):
        q_ref[:, j * hd:(j + 1) * hd] = (norm_rope(qkv[:, j * hd:(j + 1) * hd], qg_ref[...]) * qscale).astype(BF16)
    for j in range(nk):
        c0 = (nq + j) * hd
        k_ref[:, j * hd:(j + 1) * hd] = norm_rope(qkv[:, c0:c0 + hd], kg_ref[...]).astype(BF16)
    v_ref[...] = qkv[:, (nq + nk) * hd:].astype(BF16)


def _attn_qkv(xa, m, g, w, qg, kg, cosf, sinf, *, n_tiles, tpb, nb, qscale):
    r, d = xa.shape
    hd = qg.shape[1]
    nq, nk = ATT_HEADS * hd, ATT_KV_HEADS * hd
    tab_map = lambda i: (jnp.where(i < nb * tpb, i % tpb, tpb), 0)
    return pl.pallas_call(
        functools.partial(_attn_qkv_kernel, tpb=tpb, nb=nb, qscale=qscale),
        out_shape=(jax.ShapeDtypeStruct((r, nq), BF16), jax.ShapeDtypeStruct((r, nk), BF16),
                   jax.ShapeDtypeStruct((r, nk), BF16)),
        grid=(n_tiles,),
        in_specs=[pl.BlockSpec((ROW_TILE, d), lambda i: (i, 0)),
                  pl.BlockSpec(m.shape, lambda i: (0, 0)),
                  pl.BlockSpec((1, d), lambda i: (0, 0)),
                  pl.BlockSpec(w.shape, lambda i: (0, 0)),
                  pl.BlockSpec((1, hd), lambda i: (0, 0)),
                  pl.BlockSpec((1, hd), lambda i: (0, 0)),
                  pl.BlockSpec((ROW_TILE, hd), tab_map),
                  pl.BlockSpec((ROW_TILE, hd), tab_map)],
        out_specs=(pl.BlockSpec((ROW_TILE, nq), lambda i: (i, 0)),
                   pl.BlockSpec((ROW_TILE, nk), lambda i: (i, 0)),
                   pl.BlockSpec((ROW_TILE, nk), lambda i: (i, 0))),
        compiler_params=_cparams(("parallel",)),
        name="attn_qkv",
    )(xa, m, g, w, qg, kg, cosf, sinf)


def _flash_kernel(*refs, tq, tk, hd, n_lat_chunks):
    if n_lat_chunks:
        q_ref, kc_ref, vc_ref, kl_ref, vl_ref, o_ref, sa_ref, sb_ref = refs
    else:
        q_ref, kc_ref, vc_ref, o_ref = refs
    qs = jnp.concatenate([q_ref[:, g * hd:(g + 1) * hd] for g in range(ATT_GROUP)], axis=0)
    rows = ATT_GROUP * tq

    def scores(kblk):
        return lax.dot_general(qs, kblk, (((1,), (1,)), ((), ())), preferred_element_type=F32)

    def absorb(s, vblk, carry):
        m, l, acc = carry
        m_new = jnp.maximum(m, jnp.max(s, axis=-1, keepdims=True))
        p = jnp.exp2((s - m_new).astype(BF16))
        alpha = jnp.exp2(m - m_new)
        psum = functools.reduce(jnp.add, [p[:, i * LANES:(i + 1) * LANES] for i in range(p.shape[1] // LANES)])
        l = alpha * l + jnp.sum(psum.astype(F32), axis=-1, keepdims=True)
        acc = alpha * acc + jnp.dot(p, vblk, preferred_element_type=F32)
        return m_new, l, acc

    def lat(ref, c):
        return ref[pl.ds(pl.multiple_of(c * tk, tk), tk), :]

    carry = (jnp.full((rows, 1), -jnp.inf, F32), jnp.zeros((rows, 1), F32), jnp.zeros((rows, hd), F32))
    carry = absorb(scores(kc_ref[...]), vc_ref[...], carry)
    def lat_scores(c):
        return jnp.dot(qs, kl_ref[c], preferred_element_type=F32)

    if n_lat_chunks == 1:
        carry = absorb(lat_scores(0), lat(vl_ref, 0), carry)
    elif n_lat_chunks:
        assert n_lat_chunks % 2 == 0
        sa_ref[...] = lat_scores(0)

        def body(c2, carry):
            c = 2 * c2
            sb_ref[...] = lat_scores(c + 1)
            carry = absorb(sa_ref[...], lat(vl_ref, c), carry)
            sa_ref[...] = lat_scores(jnp.minimum(c + 2, n_lat_chunks - 1))
            return absorb(sb_ref[...], lat(vl_ref, c + 1), carry)

        carry = lax.fori_loop(0, n_lat_chunks // 2, body, carry, unroll=8 if n_lat_chunks % 16 == 0 else 1)
    _, l, acc = carry
    o = acc / l
    for g in range(ATT_GROUP):
        o_ref[:, g * hd:(g + 1) * hd] = o[g * tq:(g + 1) * tq].astype(BF16)


def _flash(q, k, v, *, nb, t_lat, n_ctx, hd, lat_queries):
    gw = ATT_GROUP * hd
    ctx_base = nb * t_lat // n_ctx
    kv_ctx = pl.BlockSpec((n_ctx, hd), lambda b, h, i: (ctx_base + b, h))
    if lat_queries:
        tq = min(256, t_lat)
        tk = min(1024, t_lat)
        nq_tiles = t_lat // tq
        q_spec = pl.BlockSpec((tq, gw), lambda b, h, i: (b * nq_tiles + i, h))
        n_chunks = t_lat // tk
        kv_lat = pl.BlockSpec((t_lat, hd), lambda b, h, i: (b, h), pipeline_mode=pl.Buffered(1))
        kt = k[:nb * t_lat].reshape(nb, n_chunks, tk, ATT_KV_HEADS, hd).transpose(0, 3, 1, 4, 2)
        kt_lat = pl.BlockSpec((None, None, n_chunks, hd, tk), lambda b, h, i: (b, h, 0, 0, 0),
                              pipeline_mode=pl.Buffered(1))
        in_specs = [q_spec, kv_ctx, kv_ctx, kt_lat, kv_lat]
        args = (q, k, v, kt, v)
        out_rows = nb * t_lat
    else:
        tq = n_ctx
        tk = n_ctx
        nq_tiles = 1
        q_spec = pl.BlockSpec((tq, gw), lambda b, h, i: (ctx_base + b, h))
        in_specs = [q_spec, kv_ctx, kv_ctx]
        args = (q, k, v)
        n_chunks = 0
        out_rows = nb * n_ctx
    return pl.pallas_call(
        functools.partial(_flash_kernel, tq=tq, tk=tk, hd=hd, n_lat_chunks=n_chunks),
        out_shape=jax.ShapeDtypeStruct((out_rows, q.shape[1]), BF16),
        grid=(nb, ATT_KV_HEADS, nq_tiles),
        in_specs=in_specs,
        out_specs=pl.BlockSpec((tq, gw), lambda b, h, i: (b * nq_tiles + i, h)),
        scratch_shapes=[pltpu.VMEM((ATT_GROUP * tq, tk), F32)] * 2 if n_chunks else [],
        compiler_params=_cparams(("parallel", "parallel", "parallel")),
        name="flash_lat" if lat_queries else "flash_ctx",
    )(*args)


def _rope_tables(t_lat, hd):
    rows = t_lat // GRID_W
    r_idx, c_idx = jnp.meshgrid(jnp.arange(rows, dtype=F32), jnp.arange(GRID_W, dtype=F32), indexing='ij')
    pairs = hd // 4
    freqs = ROPE_THETA ** (-jnp.arange(pairs, dtype=F32) / pairs)
    ang = jnp.concatenate([r_idx.reshape(-1, 1) * freqs, c_idx.reshape(-1, 1) * freqs], axis=-1)
    cos, sin = jnp.cos(ang), jnp.sin(ang)
    cosf = jnp.concatenate([cos, cos], axis=-1)
    sinf = jnp.concatenate([-sin, sin], axis=-1)
    cosf = jnp.concatenate([cosf, jnp.ones((ROW_TILE, hd), F32)], axis=0)
    sinf = jnp.concatenate([sinf, jnp.zeros((ROW_TILE, hd), F32)], axis=0)
    return cosf, sinf


def _deinterleave_perm(hd):
    return np.concatenate([np.arange(0, hd, 2), np.arange(1, hd, 2)])


def _attention_layer(xa, m, g, w_in, q_gain, k_gain, w_out, geo, ctx_out):
    nb, t_lat, n_ctx, tpb, all_tiles, n_tiles = geo
    hd = q_gain.shape[0]
    perm = _deinterleave_perm(hd)
    nqk = (ATT_HEADS + ATT_KV_HEADS) * hd
    col = np.concatenate([(np.arange(nqk) // hd) * hd + perm[np.arange(nqk) % hd],
                          np.arange(nqk, w_in.shape[1])])
    w = w_in[:, col].astype(BF16)
    cosf, sinf = _rope_tables(t_lat, hd)
    qscale = hd ** -0.5 * math.log2(math.e)
    q, k, v = _attn_qkv(xa, m, g, w, q_gain[perm].reshape(1, hd), k_gain[perm].reshape(1, hd), cosf, sinf,
                        n_tiles=all_tiles, tpb=tpb, nb=nb, qscale=qscale)
    o = _flash(q, k, v, nb=nb, t_lat=t_lat, n_ctx=n_ctx, hd=hd, lat_queries=True)
    if ctx_out:
        o_c = _flash(q, k, v, nb=nb, t_lat=t_lat, n_ctx=n_ctx, hd=hd, lat_queries=False)
        o = jnp.concatenate([o, o_c], axis=0)
    return _resid_matmul(xa, o, w_out.astype(BF16), m, n_tiles=n_tiles, tpb=tpb, nb=nb, k_gate=2)


def _seq_block(b, c, *, nb, nc_ctx, nc_lat, reverse):
    ctx_base = nb * nc_lat
    in_ctx = c < nc_ctx
    cl = c - nc_ctx
    if reverse:
        return jnp.where(in_ctx, ctx_base + b * nc_ctx + (nc_ctx - 1 - c), b * nc_lat + (nc_lat - 1 - cl))
    return jnp.where(in_ctx, ctx_base + b * nc_ctx + c, b * nc_lat + cl)


def _roll_rows(x, s, reverse):
    return pltpu.roll(x, SUBLANES - s if reverse else s, 0)


def _last_row(x, reverse):
    return jnp.broadcast_to(x[0:1, :] if reverse else x[SUBLANES - 1:SUBLANES, :], x.shape)


def _s5_scan_kernel(u_ref, bd_ref, tab_ref, cd_ref, y_ref, hs_ref, st_ref, *, tc, ns, reverse):
    c = pl.program_id(1)

    @pl.when(c == 0)
    def _():
        st_ref[...] = jnp.zeros_like(st_ref)

    wh, sh = u_ref.shape[1] // 2, ns // 2
    for g in range(2):
        ub = u_ref[:, g * wh:(g + 1) * wh].astype(BF16)
        for part in range(2):
            cols = slice(part * ns + g * sh, part * ns + (g + 1) * sh)
            hs_ref[:, cols] = jnp.dot(ub, bd_ref[0, g * wh:(g + 1) * wh, cols], preferred_element_type=F32)
    ng = tc // SUBLANES

    def body(gi, carry):
        hr, hi = carry
        r0 = ((ng - 1 - gi) if reverse else gi) * SUBLANES
        br = hs_ref[pl.ds(r0, SUBLANES), 0:ns]
        bi = hs_ref[pl.ds(r0, SUBLANES), ns:2 * ns]
        for k, s in enumerate((1, 2, 4)):
            rr, ri = _roll_rows(br, s, reverse), _roll_rows(bi, s, reverse)
            mr, mi = tab_ref[0, 2 * k], tab_ref[0, 2 * k + 1]
            br, bi = br + mr * rr - mi * ri, bi + mr * ri + mi * rr
        pr, pi_ = tab_ref[0, 6], tab_ref[0, 7]
        nr = br + pr * hr - pi_ * hi
        ni = bi + pr * hi + pi_ * hr
        hs_ref[pl.ds(r0, SUBLANES), 0:ns] = nr
        hs_ref[pl.ds(r0, SUBLANES), ns:2 * ns] = ni
        return _last_row(nr, reverse), _last_row(ni, reverse)

    carry = (st_ref[:, 0:ns], st_ref[:, ns:2 * ns])
    for gi in range(ng):
        carry = body(gi, carry)
    hr, hi = carry
    st_ref[:, 0:ns] = hr
    st_ref[:, ns:2 * ns] = hi
    for g in range(2):
        y_ref[:, g * wh:(g + 1) * wh] = sum(
            jnp.dot(hs_ref[:, part * ns + g * sh:part * ns + (g + 1) * sh].astype(BF16),
                    cd_ref[0, part * ns + g * sh:part * ns + (g + 1) * sh, g * wh:(g + 1) * wh],
                    preferred_element_type=F32) for part in range(2))


def _s5_scan(u, bd, tab, cd, geo, direction):
    nb, t_lat, n_ctx = geo[:3]
    r, width = u.shape
    ns = tab.shape[3]
    tc = SEQ_TILE
    nc_ctx, nc_lat = n_ctx // tc, t_lat // tc
    blk = functools.partial(_seq_block, nb=nb, nc_ctx=nc_ctx, nc_lat=nc_lat, reverse=direction == 1)
    return pl.pallas_call(
        functools.partial(_s5_scan_kernel, tc=tc, ns=ns, reverse=direction == 1),
        out_shape=jax.ShapeDtypeStruct((r, width), F32),
        grid=(nb, nc_ctx + nc_lat),
        in_specs=[pl.BlockSpec((tc, width), lambda b, c: (blk(b, c), 0)),
                  pl.BlockSpec((1, width, 2 * ns), lambda b, c: (direction, 0, 0)),
                  pl.BlockSpec((1, 8, SUBLANES, ns), lambda b, c: (direction, 0, 0, 0)),
                  pl.BlockSpec((1, 2 * ns, width), lambda b, c: (direction, 0, 0))],
        out_specs=pl.BlockSpec((tc, width), lambda b, c: (blk(b, c), 0)),
        scratch_shapes=[pltpu.VMEM((tc, 2 * ns), F32), pltpu.VMEM((SUBLANES, 2 * ns), F32)],
        compiler_params=_cparams(("arbitrary", "arbitrary")),
        name="s5_scan_bwd" if direction else "s5_scan_fwd",
    )(u, bd, tab, cd)


def _s5_out_kernel(x_ref, u_ref, yf_ref, yb_ref, dsk_ref, wg_ref, wo_ref, m_ref, o_ref, *, tpb, nb):
    d = x_ref.shape[1]
    sel = _row_sel(pl.program_id(0), tpb, nb)
    y = dsk_ref[...] * u_ref[...] + yf_ref[...] + yb_ref[...]
    z = _gelu(y)
    z = z * _sigmoid(jnp.dot(z.astype(BF16), wg_ref[...], preferred_element_type=F32))
    out = jnp.dot(z.astype(BF16), wo_ref[...], preferred_element_type=F32)
    o_ref[...] = x_ref[...] + _mod_row(m_ref, sel, 2, d) * out


def _s5_out(xa, u, yf, yb, dsk, wg, wo, m, *, n_tiles, tpb, nb):
    r, d = xa.shape
    width = u.shape[1]
    return pl.pallas_call(
        functools.partial(_s5_out_kernel, tpb=tpb, nb=nb),
        out_shape=jax.ShapeDtypeStruct((r, d), F32),
        grid=(n_tiles,),
        in_specs=[pl.BlockSpec((ROW_TILE, d), lambda i: (i, 0)),
                  pl.BlockSpec((ROW_TILE, width), lambda i: (i, 0)),
                  pl.BlockSpec((ROW_TILE, width), lambda i: (i, 0)),
                  pl.BlockSpec((ROW_TILE, width), lambda i: (i, 0)),
                  pl.BlockSpec((1, width), lambda i: (0, 0)),
                  pl.BlockSpec(wg.shape, lambda i: (0, 0)),
                  pl.BlockSpec(wo.shape, lambda i: (0, 0)),
                  pl.BlockSpec(m.shape, lambda i: (0, 0))],
        out_specs=pl.BlockSpec((ROW_TILE, d), lambda i: (i, 0)),
        input_output_aliases={0: 0},
        compiler_params=_cparams(("parallel",)),
        name="s5_out",
    )(xa, u, yf, yb, dsk, wg, wo, m)


def _s5_layer(xa, m, g, w_in, lam_re, lam_im, log_dt, b_re, b_im, c_re, c_im, d_skip, w_glu, w_out, geo):
    nb, t_lat, n_ctx, tpb, all_tiles, n_tiles = geo
    groups, p_state = lam_re.shape[1], lam_re.shape[2]
    width = w_in.shape[1]
    ns = groups * p_state
    lam = lax.complex(lam_re.astype(F32), lam_im.astype(F32))
    dt = jnp.exp(log_dt.astype(F32))[..., None]
    lam_bar = jnp.exp(lam * dt)
    b_bar = ((lam_bar - 1.0) / lam)[..., None] * lax.complex(b_re.astype(F32), b_im.astype(F32))
    eye = jnp.eye(groups, dtype=F32)

    def block_diag_in(z):
        return jnp.einsum('dgpc,gh->dgchp', z, eye).reshape(2, groups * SSM_GROUP, ns)

    def block_diag_out(z):
        return jnp.einsum('dgcp,gh->dgphc', z, eye).reshape(2, ns, groups * SSM_GROUP)

    bd = jnp.concatenate([block_diag_in(jnp.real(b_bar)), block_diag_in(jnp.imag(b_bar))], axis=-1).astype(BF16)
    cd = jnp.concatenate([block_diag_out(c_re.astype(F32)), block_diag_out(-c_im.astype(F32))], axis=1).astype(BF16)
    pw = [lam_bar.reshape(2, ns)]
    for _ in range(SUBLANES - 1):
        pw.append(pw[-1] * pw[0])
    row = jnp.arange(SUBLANES)
    tabs = []
    for dirn in range(2):
        rev = dirn == 1
        planes = []
        for s in (1, 2, 4):
            keep = (row + s <= SUBLANES - 1) if rev else (row >= s)
            ms = jnp.where(keep[:, None], pw[s - 1][dirn][None, :], 0.0)
            planes += [jnp.real(ms), jnp.imag(ms)]
        pows = jnp.stack([pw[(SUBLANES - 1 - t) if rev else t][dirn] for t in range(SUBLANES)], axis=0)
        planes += [jnp.real(pows), jnp.imag(pows)]
        tabs.append(jnp.stack(planes, axis=0))
    tab = jnp.stack(tabs, axis=0).astype(F32)

    u = _norm_matmul(xa, m, g, w_in.astype(BF16), n_tiles=all_tiles, tpb=tpb, nb=nb, k_shift=0)
    yf = _s5_scan(u, bd, tab, cd, geo, 0)
    yb = _s5_scan(u, bd, tab, cd, geo, 1)
    return _s5_out(xa, u, yf, yb, d_skip.reshape(1, width).astype(F32), w_glu.astype(BF16), w_out.astype(BF16), m,
                   n_tiles=n_tiles, tpb=tpb, nb=nb)


def _lru_coef_kernel(cur_ref, prev_ref, next_ref, cw_ref, cb_ref, wrg_ref, brg_ref, wig_ref, big_ref, cn_ref,
                     a_ref, b_ref, buf_ref, *, tm, tps_lat, tps_ctx, n_lat_tiles):
    i = pl.program_id(0)
    j = i - n_lat_tiles
    first = jnp.where(i < n_lat_tiles, i % tps_lat == 0, j % tps_ctx == 0)
    last = jnp.where(i < n_lat_tiles, i % tps_lat == tps_lat - 1, j % tps_ctx == tps_ctx - 1)
    h = SUBLANES
    buf_ref[0:h, :] = jnp.where(first, 0.0, prev_ref[...])
    buf_ref[h:h + tm, :] = cur_ref[...]
    buf_ref[h + tm:h + tm + h, :] = jnp.where(last, 0.0, next_ref[...])
    xr = cb_ref[...] + sum(cw_ref[k:k + 1, :] * buf_ref[h - CONV_LEFT + k:h - CONV_LEFT + k + tm, :]
                           for k in range(CONV_W))
    nblk = wrg_ref.shape[1]
    bw = wrg_ref.shape[2]
    for n in range(nblk):
        sl = slice(n * bw, (n + 1) * bw)
        xs = xr[:, sl]
        xb = xs.astype(BF16)
        for d in range(2):
            r = _sigmoid(jnp.dot(xb, wrg_ref[d, n], preferred_element_type=F32) + brg_ref[d, :, sl])
            ig = _sigmoid(jnp.dot(xb, wig_ref[d, n], preferred_element_type=F32) + big_ref[d, :, sl])
            a = jnp.exp(cn_ref[d, :, sl] * r)
            a_ref[d, :, sl] = a
            b_ref[d, :, sl] = jnp.sqrt(1.0 - a * a) * (ig * xs)


def _lru_coef(gx, conv_w, conv_b, wrg, brg, wig, big, cneg, geo):
    nb, t_lat, n_ctx = geo[:3]
    r = gx.shape[0]
    width = conv_w.shape[1]
    tm = SEQ_TILE
    hb = tm // SUBLANES
    n_lat_tiles = nb * t_lat // tm
    last_halo = r // SUBLANES - 1
    vec = lambda a: pl.BlockSpec(a.shape, lambda i: (0,) * a.ndim)
    return pl.pallas_call(
        functools.partial(_lru_coef_kernel, tm=tm, tps_lat=t_lat // tm, tps_ctx=n_ctx // tm, n_lat_tiles=n_lat_tiles),
        out_shape=(jax.ShapeDtypeStruct((2, r, width), F32), jax.ShapeDtypeStruct((2, r, width), F32)),
        grid=(r // tm,),
        in_specs=[pl.BlockSpec((tm, width), lambda i: (i, 1)),
                  pl.BlockSpec((SUBLANES, width), lambda i: (jnp.maximum(i * hb - 1, 0), 1)),
                  pl.BlockSpec((SUBLANES, width), lambda i: (jnp.minimum((i + 1) * hb, last_halo), 1)),
                  vec(conv_w), vec(conv_b), vec(wrg), vec(brg), vec(wig), vec(big), vec(cneg)],
        out_specs=(pl.BlockSpec((2, tm, width), lambda i: (0, i, 0)),
                   pl.BlockSpec((2, tm, width), lambda i: (0, i, 0))),
        scratch_shapes=[pltpu.VMEM((tm + 2 * SUBLANES, width), F32)],
        compiler_params=_cparams(("parallel",)),
        name="lru_coef",
    )(gx, gx, gx, conv_w, conv_b, wrg, brg, wig, big, cneg)


def _lru_scan_kernel(a_ref, b_ref, o_ref, st_ref, *, tc, reverse):
    c = pl.program_id(1)

    @pl.when(c == 0)
    def _():
        st_ref[...] = jnp.zeros_like(st_ref)

    width = a_ref.shape[-1]
    row = lax.broadcasted_iota(jnp.int32, (SUBLANES, width), 0)
    ng = tc // SUBLANES

    def body(gi, h):
        r0 = pl.multiple_of(((ng - 1 - gi) if reverse else gi) * SUBLANES, SUBLANES)
        a = a_ref[0, pl.ds(r0, SUBLANES), :]
        b = b_ref[0, pl.ds(r0, SUBLANES), :]
        for s in (1, 2, 4):
            keep = (row < SUBLANES - s) if reverse else (row >= s)
            ar = jnp.where(keep, _roll_rows(a, s, reverse), 1.0)
            br = jnp.where(keep, _roll_rows(b, s, reverse), 0.0)
            b = a * br + b
            a = a * ar
        hn = a * h + b
        o_ref[pl.ds(r0, SUBLANES), :] = hn
        return _last_row(hn, reverse)

    st_ref[...] = lax.fori_loop(0, ng, body, st_ref[...], unroll=8)


def _lru_scan(a, b, geo, direction):
    nb, t_lat, n_ctx = geo[:3]
    _, r, width = a.shape
    tc = SEQ_TILE
    nc_ctx, nc_lat = n_ctx // tc, t_lat // tc
    blk = functools.partial(_seq_block, nb=nb, nc_ctx=nc_ctx, nc_lat=nc_lat, reverse=direction == 1)
    spec = pl.BlockSpec((1, tc, width), lambda b_, c: (direction, blk(b_, c), 0))
    return pl.pallas_call(
        functools.partial(_lru_scan_kernel, tc=tc, reverse=direction == 1),
        out_shape=jax.ShapeDtypeStruct((r, width), F32),
        grid=(nb, nc_ctx + nc_lat),
        in_specs=[spec, spec],
        out_specs=pl.BlockSpec((tc, width), lambda b_, c: (blk(b_, c), 0)),
        scratch_shapes=[pltpu.VMEM((SUBLANES, width), F32)],
        compiler_params=_cparams(("arbitrary", "arbitrary")),
        name="lru_scan_bwd" if direction else "lru_scan_fwd",
    )(a, b)


def _lru_out_kernel(x_ref, gate_ref, sf_ref, sb_ref, wo_ref, m_ref, o_ref, *, tpb, nb):
    d = x_ref.shape[1]
    sel = _row_sel(pl.program_id(0), tpb, nb)
    y = (sf_ref[...] + sb_ref[...]) * _gelu(gate_ref[...])
    out = jnp.dot(y.astype(BF16), wo_ref[...], preferred_element_type=F32)
    o_ref[...] = x_ref[...] + _mod_row(m_ref, sel, 2, d) * out


def _lru_out(xa, gx, sf, sb, wo, m, *, n_tiles, tpb, nb):
    r, d = xa.shape
    width = sf.shape[1]
    return pl.pallas_call(
        functools.partial(_lru_out_kernel, tpb=tpb, nb=nb),
        out_shape=jax.ShapeDtypeStruct((r, d), F32),
        grid=(n_tiles,),
        in_specs=[pl.BlockSpec((ROW_TILE, d), lambda i: (i, 0)),
                  pl.BlockSpec((ROW_TILE, width), lambda i: (i, 0)),
                  pl.BlockSpec((ROW_TILE, width), lambda i: (i, 0)),
                  pl.BlockSpec((ROW_TILE, width), lambda i: (i, 0)),
                  pl.BlockSpec(wo.shape, lambda i: (0, 0)),
                  pl.BlockSpec(m.shape, lambda i: (0, 0))],
        out_specs=pl.BlockSpec((ROW_TILE, d), lambda i: (i, 0)),
        input_output_aliases={0: 0},
        compiler_params=_cparams(("parallel",)),
        name="lru_out",
    )(xa, gx, sf, sb, wo, m)


def _lru_layer(xa, m, g, w_in, conv_w, conv_b, w_rg, b_rg, w_ig, b_ig, lam, w_out, geo):
    nb, t_lat, n_ctx, tpb, all_tiles, n_tiles = geo
    width = conv_w.shape[1]
    gx = _norm_matmul(xa, m, g, w_in.astype(BF16), n_tiles=all_tiles, tpb=tpb, nb=nb, k_shift=0)
    cneg = (-LRU_C * jax.nn.softplus(-lam.astype(F32))).reshape(2, 1, width)
    a, b = _lru_coef(gx, conv_w.astype(F32), conv_b.reshape(1, width).astype(F32),
                     w_rg.astype(BF16), b_rg.reshape(2, 1, width).astype(F32),
                     w_ig.astype(BF16), b_ig.reshape(2, 1, width).astype(F32), cneg, geo)
    sf = _lru_scan(a, b, geo, 0)
    sb = _lru_scan(a, b, geo, 1)
    return _lru_out(xa, gx, sf, sb, w_out.astype(BF16), m, n_tiles=n_tiles, tpb=tpb, nb=nb)


def _oddeven_merge_sort_pairs(n):
    pairs = []
    p = 1
    while p < n:
        k = p
        while k >= 1:
            for j in range(k % p, n - k, 2 * k):
                for i in range(min(k, n - j - k)):
                    if (i + j) // (2 * p) == (i + j + k) // (2 * p):
                        pairs.append((i + j, i + j + k))
            k //= 2
        p *= 2
    return pairs


_SORT16 = _oddeven_merge_sort_pairs(PEER_TOPK)


def _ce(v, i, j):
    a, b = v[i], v[j]
    v[i] = jnp.maximum(a, b)
    v[j] = jnp.minimum(a, b)


def _bitonic_merge_desc(v):
    n = len(v)
    k = n // 2
    while k >= 1:
        for i in range(n):
            if i & k == 0:
                _ce(v, i, i | k)
        k //= 2


def _top16_desc(blk):
    n = PEER_TOPK
    v = [blk[SUBLANES * a:SUBLANES * (a + 1), :] for a in range(n)]
    for (i, j) in _SORT16:
        _ce(v, i, j)
    for shift in (4, 2, 1):
        o = [pltpu.roll(x, shift, 0) for x in v]
        v = [jnp.maximum(v[i], o[n - 1 - i]) for i in range(n)]
        _bitonic_merge_desc(v)
    return v


_KEY_FLOOR = 2.0 ** -100


def _peer_route(s, nk):
    nh, k = PEER_HEADS, PEER_TOPK
    w = s.shape[1]
    low = nk - 1
    key = lax.broadcasted_iota(jnp.int32, (nk, w), 0)
    sub = lax.broadcasted_iota(jnp.int32, (SUBLANES, w), 0)
    vals, idxs = [], []
    for half in range(2):
        pk = [jnp.zeros((SUBLANES, w), F32) for _ in range(k)]
        for hh in range(nh):
            blk = s[(half * nh + hh) * nk:(half * nh + hh + 1) * nk, :]
            bits = lax.bitcast_convert_type(jnp.where(jnp.abs(blk) < _KEY_FLOOR, _KEY_FLOOR, blk), jnp.int32)
            neg = (bits >> 31) & low
            keys = lax.bitcast_convert_type((bits & ~low) | ((low - key) ^ neg), F32)
            top = _top16_desc(keys)
            pk = [jnp.where(sub == hh, top[i], pk[i]) for i in range(k)]
        pkb = [lax.bitcast_convert_type(x, jnp.int32) for x in pk]
        idxs.append([low - ((x & low) ^ ((x >> 31) & low)) for x in pkb])
        vals.append([lax.bitcast_convert_type(x & ~low, F32) for x in pkb])
    cbits = k * k - 1

    def pack(v, code):
        b = lax.bitcast_convert_type(jnp.where(jnp.abs(v) < _KEY_FLOOR, _KEY_FLOOR, v), jnp.int32)
        return lax.bitcast_convert_type((b & ~cbits) | ((cbits - code) ^ ((b >> 31) & cbits)), F32)

    top = [pack(vals[0][0] + vals[1][j], j) for j in range(k)]
    for i in range(1, k):
        n = k // (i + 1)
        for t in range(n):
            top[k - 1 - t] = jnp.maximum(top[k - 1 - t], pack(vals[0][i] + vals[1][t], i * k + t))
        _bitonic_merge_desc(top)
    tb = [lax.bitcast_convert_type(x, jnp.int32) for x in top]
    codes = [cbits - ((x & cbits) ^ ((x >> 31) & cbits)) for x in tb]
    best = [lax.bitcast_convert_type(x & ~cbits, F32) for x in tb]
    ex = [jnp.exp(b - best[0]) for b in best]
    zinv = 1.0 / functools.reduce(lambda a, b: a + b, ex)
    i1, i2 = [], []
    for hc in codes:
        ri, rj = hc >> (k.bit_length() - 1), hc & (k - 1)
        i1.append(functools.reduce(lambda a, b: a + b, [jnp.where(ri == i, idxs[0][i], 0) for i in range(k)]))
        i2.append(functools.reduce(lambda a, b: a + b, [jnp.where(rj == j, idxs[1][j], 0) for j in range(k)]))
    cat = lambda xs: jnp.concatenate(xs, axis=0)
    return cat(i1), cat(i2), cat([e * zinv for e in ex])


def _peer_score_kernel(x_ref, m_ref, g_ref, wq_ref, kk_ref, h_ref, i1_ref, i2_ref, gate_ref, s_ref, *, tpb, nb):
    d = x_ref.shape[1]
    nk = kk_ref.shape[1]
    kd = kk_ref.shape[2]
    sel = _row_sel(pl.program_id(0), tpb, nb)
    h = _modnorm(x_ref[...], g_ref[...], _mod_row(m_ref, sel, 3, d), _mod_row(m_ref, sel, 4, d)).astype(BF16)
    h_ref[...] = h
    q = jnp.dot(h, wq_ref[...], preferred_element_type=F32).astype(BF16)
    for half in range(2):
        for hh in range(PEER_HEADS):
            jq = hh * 2 + half
            blk = half * PEER_HEADS + hh
            s_ref[blk * nk:(blk + 1) * nk, :] = lax.dot_general(
                kk_ref[blk], q[:, jq * kd:(jq + 1) * kd], (((1,), (1,)), ((), ())),
                preferred_element_type=F32)
    i1, i2, gate = _peer_route(s_ref, nk)
    i1_ref[...] = i1.astype(F32).T
    i2_ref[...] = i2.astype(F32).T
    gate_ref[...] = gate.T


def _peer_scores(xa, m, g, wq, kk, *, n_tiles, tpb, nb):
    r, d = xa.shape
    nk = kk.shape[1]
    slots = PEER_HEADS * PEER_TOPK
    slot_spec = pl.BlockSpec((ROW_TILE, slots), lambda i: (i, 0))
    slot_shape = jax.ShapeDtypeStruct((r, slots), F32)
    return pl.pallas_call(
        functools.partial(_peer_score_kernel, tpb=tpb, nb=nb),
        out_shape=(jax.ShapeDtypeStruct((r, d), BF16), slot_shape, slot_shape, slot_shape),
        grid=(n_tiles,),
        in_specs=[pl.BlockSpec((ROW_TILE, d), lambda i: (i, 0)),
                  pl.BlockSpec(m.shape, lambda i: (0, 0)),
                  pl.BlockSpec((1, d), lambda i: (0, 0)),
                  pl.BlockSpec(wq.shape, lambda i: (0, 0)),
                  pl.BlockSpec(kk.shape, lambda i: (0, 0, 0))],
        out_specs=(pl.BlockSpec((ROW_TILE, d), lambda i: (i, 0)), slot_spec, slot_spec, slot_spec),
        scratch_shapes=[pltpu.VMEM((2 * PEER_HEADS * nk, ROW_TILE), F32)],
        compiler_params=_cparams(("parallel",)),
        name="peer_scores",
    )(xa, m, g, wq, kk)


PEER_EXPERT_BLOCK = 2048


def _peer_dense_kernel(x_ref, h_ref, i1_ref, i2_ref, gate_ref, ut_ref, v_ref, m_ref, o_ref,
                       acc_ref, w_ref, *, tpb, nb, nk, pitch):
    d = x_ref.shape[1]
    tm = x_ref.shape[0]
    eb = ut_ref.shape[1]
    n_i1 = eb // nk
    rows = nk // 2
    steps = rows // n_i1
    j = pl.program_id(1)
    hi_mask = jnp.int32(-65536)

    @pl.when(j == 0)
    def _():
        acc_ref[...] = jnp.zeros_like(acc_ref)
        slots = i1_ref.shape[1]
        key = lax.broadcasted_iota(jnp.int32, (nk, slots), 0).astype(F32)

        def chunk(ci, carry):
            r0 = pl.multiple_of(ci * SUBLANES, SUBLANES)
            i1c = i1_ref[pl.ds(r0, SUBLANES), :]
            i2c = i2_ref[pl.ds(r0, SUBLANES), :]
            gc = gate_ref[pl.ds(r0, SUBLANES), :]
            for r in range(SUBLANES):
                lhs = jnp.where(key == i1c[r:r + 1, :], gc[r:r + 1, :], 0.0).astype(BF16)
                rhs = jnp.where(key == i2c[r:r + 1, :], 1.0, 0.0).astype(BF16)
                wt = lax.dot_general(lhs, rhs, (((1,), (1,)), ((), ())), preferred_element_type=F32)
                bits = lax.bitcast_convert_type(wt, jnp.int32) + 0x8000
                word = (bits[rows:] & hi_mask) | lax.shift_right_logical(bits[:rows], 16)
                w_ref[pl.ds(pl.multiple_of((r0 + r) * pitch, SUBLANES), rows), :] = word
            return carry

        lax.fori_loop(0, tm // SUBLANES, chunk, 0, unroll=8)

    s = jnp.dot(h_ref[...], ut_ref[...], preferred_element_type=F32)
    jj = j % steps
    shift = jnp.where(j < steps, 16, 0)
    p = []
    for ii in range(n_i1):
        word = w_ref[pl.ds(jj * n_i1 + ii, tm, stride=pitch), :]
        wb = lax.bitcast_convert_type(lax.shift_left(word, jnp.broadcast_to(shift, word.shape)) & hi_mask, F32)
        p.append((wb * _gelu(s[:, ii * nk:(ii + 1) * nk])).astype(BF16))
    acc_ref[...] += jnp.dot(jnp.concatenate(p, axis=1), v_ref[...], preferred_element_type=F32)

    @pl.when(j == pl.num_programs(1) - 1)
    def _():
        sel = _row_sel(pl.program_id(0), tpb, nb)
        o_ref[...] = x_ref[...] + _mod_row(m_ref, sel, 5, d) * acc_ref[...]


def _peer_dense(xa, h, i1, i2, gate, ut, v, m, *, n_tiles, tpb, nb, nk):
    r, d = xa.shape
    n_exp = v.shape[0]
    eb = PEER_EXPERT_BLOCK
    tm = ROW_TILE
    slots = i1.shape[1]
    rows = nk // 2
    assert rows % (eb // nk) == 0 and n_exp == nk * nk
    pitch = rows + SUBLANES if (rows // SUBLANES) % 2 == 0 else rows
    slot_spec = pl.BlockSpec((tm, slots), lambda i, j: (i, 0))
    return pl.pallas_call(
        functools.partial(_peer_dense_kernel, tpb=tpb, nb=nb, nk=nk, pitch=pitch),
        out_shape=jax.ShapeDtypeStruct((r, d), F32),
        grid=(n_tiles, n_exp // eb),
        in_specs=[pl.BlockSpec((tm, d), lambda i, j: (i, 0)),
                  pl.BlockSpec((tm, d), lambda i, j: (i, 0)),
                  slot_spec, slot_spec, slot_spec,
                  pl.BlockSpec((d, eb), lambda i, j: (0, j)),
                  pl.BlockSpec((eb, d), lambda i, j: (j, 0)),
                  pl.BlockSpec(m.shape, lambda i, j: (0, 0))],
        out_specs=pl.BlockSpec((tm, d), lambda i, j: (i, 0)),
        scratch_shapes=[pltpu.VMEM((tm, d), F32),
                        pltpu.VMEM((tm * pitch, nk), jnp.int32)],
        input_output_aliases={0: 0},
        compiler_params=pltpu.CompilerParams(dimension_semantics=("parallel", "arbitrary"),
                                             vmem_limit_bytes=PEER_VMEM_LIMIT),
        name="peer_dense",
    )(xa, h, i1, i2, gate, ut, v, m)


def _peer_layer(xa, m, g, w_q, k1, k2, u_tab, v_tab, geo):
    nb, t_lat, n_ctx, tpb, all_tiles, n_tiles = geo
    nk = k1.shape[1]
    kk = jnp.concatenate([k1, k2], axis=0).astype(BF16)
    h, i1, i2, gate = _peer_scores(xa, m, g, w_q.astype(BF16), kk, n_tiles=n_tiles, tpb=tpb, nb=nb)
    return _peer_dense(xa, h, i1, i2, gate, u_tab.astype(BF16).T, v_tab.astype(BF16), m,
                       n_tiles=n_tiles, tpb=tpb, nb=nb, nk=nk)


def _final_norm_kernel(x_ref, g_ref, o_ref):
    x = x_ref[...]
    o_ref[...] = x * lax.rsqrt(jnp.mean(x * x, axis=-1, keepdims=True) + EPS) * g_ref[...]


def _final_norm(xa, g, n_rows):
    d = xa.shape[1]
    return pl.pallas_call(
        _final_norm_kernel,
        out_shape=jax.ShapeDtypeStruct((n_rows, d), F32),
        grid=(n_rows // ROW_TILE,),
        in_specs=[pl.BlockSpec((ROW_TILE, d), lambda i: (i, 0)), pl.BlockSpec((1, d), lambda i: (0, 0))],
        out_specs=pl.BlockSpec((ROW_TILE, d), lambda i: (i, 0)),
        compiler_params=_cparams(("parallel",)),
        name="final_norm",
    )(xa, g)


def kernel(x, c, ctx, c_ctx, mod_w, mod_b, norm_mix, norm_ffn, norm_final, attn_w_in, attn_q_gain, attn_k_gain, attn_w_out, ssm_w_in, ssm_lam_re, ssm_lam_im, ssm_log_dt, ssm_b_re, ssm_b_im, ssm_c_re, ssm_c_im, ssm_d, ssm_w_glu, ssm_w_out, lru_w_in, lru_conv_w, lru_conv_b, lru_w_rg, lru_b_rg, lru_w_ig, lru_b_ig, lru_lam, lru_w_out, peer_w_q, peer_k1, peer_k2, peer_u, peer_v):
    nb, t_lat, d = x.shape
    n_ctx = ctx.shape[1]
    depth = mod_w.shape[0]
    assert t_lat % ROW_TILE == 0 and (nb * n_ctx) % ROW_TILE == 0 and n_ctx % SEQ_TILE == 0
    assert nb + 1 <= SUBLANES and t_lat % GRID_W == 0
    tpb = t_lat // ROW_TILE
    lat_tiles = nb * tpb
    all_tiles = lat_tiles + nb * n_ctx // ROW_TILE

    xa = jnp.concatenate([x.reshape(nb * t_lat, d), ctx.reshape(nb * n_ctx, d)], axis=0).astype(F32)
    crows = jnp.concatenate([c.astype(F32), c_ctx.reshape(1, d).astype(F32),
                             jnp.zeros((SUBLANES - nb - 1, d), F32)], axis=0)
    mods = _modulation(crows, mod_w.astype(F32), mod_b.astype(F32))

    for i in range(depth):
        ctx_out = i < depth - 1
        geo = (nb, t_lat, n_ctx, tpb, all_tiles, all_tiles if ctx_out else lat_tiles)
        kind, j = i % N_MIXERS, i // N_MIXERS
        m = mods[i]
        g_mix = norm_mix[i].reshape(1, d).astype(F32)
        if kind == 0:
            xa = _attention_layer(xa, m, g_mix, attn_w_in[j], attn_q_gain[j], attn_k_gain[j], attn_w_out[j],
                                  geo, ctx_out)
        elif kind == 1:
            xa = _s5_layer(xa, m, g_mix, ssm_w_in[j], ssm_lam_re[j], ssm_lam_im[j], ssm_log_dt[j],
                           ssm_b_re[j], ssm_b_im[j], ssm_c_re[j], ssm_c_im[j], ssm_d[j],
                           ssm_w_glu[j], ssm_w_out[j], geo)
        else:
            xa = _lru_layer(xa, m, g_mix, lru_w_in[j], lru_conv_w[j], lru_conv_b[j], lru_w_rg[j], lru_b_rg[j],
                            lru_w_ig[j], lru_b_ig[j], lru_lam[j], lru_w_out[j], geo)
        xa = _peer_layer(xa, m, norm_ffn[i].reshape(1, d).astype(F32), peer_w_q[i], peer_k1[i], peer_k2[i],
                         peer_u[i], peer_v[i], geo)
    out = _final_norm(xa, norm_final.reshape(1, d).astype(F32), nb * t_lat)
    return out.reshape(nb, t_lat, d)
```
